```python
import math
import jax, jax.numpy as jnp
from jax import lax
import numpy as np

D_MODEL = 1024
BATCH = 8
SEQ = 4096
DEPTH = 2

N_EVEN = (DEPTH + 1) // 2
N_ODD = DEPTH // 2

RET_HEADS = 4
RET_DK = 128
RET_DV = 128
RET_CHUNK = 128
ROPE_THETA = 10000.0
DSA_HEADS = 8
DSA_DH = 64
IDX_HEADS = 8
IDX_DH = 32
DSA_TOPK_MAX = 256
Q_BLOCK = 128
REL_BUCKETS = 32
REL_MAX_DIST = 128
EVEN_COLS = (RET_HEADS * RET_DK, RET_HEADS * RET_DK, RET_HEADS * RET_DV, RET_HEADS * RET_DV,
             DSA_HEADS * DSA_DH, DSA_DH, DSA_DH, IDX_HEADS * IDX_DH, IDX_DH, IDX_HEADS)
EVEN_IN = sum(EVEN_COLS)
EVEN_SPLITS = [sum(EVEN_COLS[:i + 1]) for i in range(len(EVEN_COLS) - 1)]
MIX_OUT_EVEN = RET_HEADS * RET_DV + DSA_HEADS * DSA_DH
SSM_DINNER = 2 * D_MODEL
SSM_HEADDIM = 64
SSM_HEADS = SSM_DINNER // SSM_HEADDIM
SSM_GROUPS = 8
SSM_STATE = 128
SSM_CONV = 4
SSM_CHUNK = 128
CONV_DIM = SSM_DINNER + 2 * SSM_GROUPS * SSM_STATE
SSM_IN = SSM_DINNER + CONV_DIM + SSM_HEADS
N_EXPERTS = 64
TOPK = 8
D_EXPERT = 256
D_SHARED = 256
ROUTED_SCALE = 2.5
MOE_BLOCK = 128
EPS = 1e-6

kernel_name = 'hybrid_retention_dsa_ssd_moe'


def rmsnorm(x, g):
    x32 = x.astype(jnp.float32)
    y = x32 * lax.rsqrt(jnp.mean(x32 * x32, axis=-1, keepdims=True) + EPS)
    return (y * g.astype(jnp.float32)).astype(x.dtype)


def modulate(h, shift, scale):
    return h * (1.0 + scale[:, None, :]) + shift[:, None, :]


def rotary(x, pos):
    half = x.shape[-1] // 2
    inv = ROPE_THETA ** (-jnp.arange(half, dtype=jnp.float32) / half)
    ang = pos.astype(jnp.float32)[..., None] * inv
    cos = jnp.cos(ang)[:, :, None, :]
    sin = jnp.sin(ang)[:, :, None, :]
    x1, x2 = x[..., :half], x[..., half:]
    return jnp.concatenate([x1 * cos - x2 * sin, x1 * sin + x2 * cos], axis=-1).astype(x.dtype)


def t5_bucket(dist):
    n = jnp.maximum(dist, 0)
    max_exact = REL_BUCKETS // 2
    nf = jnp.maximum(n, 1).astype(jnp.float32)
    large = max_exact + (jnp.log(nf / max_exact) / math.log(REL_MAX_DIST / max_exact)
                         * (REL_BUCKETS - max_exact)).astype(jnp.int32)
    large = jnp.minimum(large, REL_BUCKETS - 1)
    return jnp.where(n < max_exact, n, large)


def retention(q, k, v, pos):
    B_, T, H, dk = q.shape
    dv = v.shape[-1]
    L = RET_CHUNK
    nC = T // L
    q = rotary(q, pos)
    k = rotary(k, pos) * dk ** -0.5
    log_g = jnp.log(1.0 - 2.0 ** (-5.0 - jnp.arange(H, dtype=jnp.float32)))
    idx = jnp.arange(L, dtype=jnp.float32)
    rel = idx[:, None] - idx[None, :]
    d_intra = jnp.where(rel >= 0, jnp.exp(log_g[:, None, None] * jnp.maximum(rel, 0.0)), 0.0)
    zeta = jnp.exp(log_g[:, None] * (L - 1 - idx))
    xi = jnp.exp(log_g[:, None] * (idx + 1))
    chunk_decay = jnp.exp(log_g * L)
    qc = q.reshape(B_, nC, L, H, dk)
    kc = k.reshape(B_, nC, L, H, dk)
    vc = v.reshape(B_, nC, L, H, dv)
    inner = jnp.einsum('bcnhd,bcmhd->bhcnm', qc, kc) * d_intra[:, None]
    y_intra = jnp.einsum('bhcnm,bcmhe->bcnhe', inner, vc)
    u = jnp.einsum('bcmhd,hm,bcmhe->cbhde', kc, zeta, vc)

    def step(r, u_c):
        return r * chunk_decay[:, None, None] + u_c, r

    _, r_prev = lax.scan(step, jnp.zeros_like(u[0]), u)
    y_cross = jnp.einsum('bcnhd,hn,cbhde->bcnhe', qc, xi, r_prev)
    return (y_intra + y_cross).reshape(B_, T, H, dv)


def dsa_attention(q, k, v, qi, ki, wi, pos, rel_bias, q_gain, k_gain):
    B_, T, H, dh = q.shape
    topk = min(DSA_TOPK_MAX, T // 4)
    nQ = T // Q_BLOCK
    q = rmsnorm(q, q_gain)
    k = rmsnorm(k, k_gain)
    wi = wi.astype(jnp.float32) * (IDX_HEADS ** -0.5 * IDX_DH ** -0.5)
    key_idx = jnp.arange(T, dtype=jnp.int32)

    def to_blocks(a):
        return a.reshape(B_, nQ, Q_BLOCK, *a.shape[2:]).swapaxes(0, 1)

    def gather_rows(table, sel):
        return jax.vmap(lambda tb, ib: tb[ib])(table, sel)

    def block(args):
        qb, qib, wib, posb, tb = args
        logit = jax.nn.relu(jnp.einsum('bqhd,bsd->bqhs', qib, ki).astype(jnp.float32))
        score = jnp.einsum('bqhs,bqh->bqs', logit, wib)
        causal = key_idx[None, None, :] <= tb[None, :, None]
        score = jnp.where(causal, score, -jnp.inf)
        _, sel = lax.top_k(score, topk)
        valid = sel <= tb[None, :, None]
        k_sel = gather_rows(k, sel)
        v_sel = gather_rows(v, sel)
        pos_sel = gather_rows(pos, sel)
        bias = rel_bias[t5_bucket(posb[:, :, None] - pos_sel)]
        logits = (jnp.einsum('bqhd,bqkd->bhqk', qb, k_sel).astype(jnp.float32) * dh ** -0.5
                  + jnp.moveaxis(bias, -1, 1).astype(jnp.float32))
        logits = jnp.where(valid[:, None], logits, -jnp.inf)
        p = jax.nn.softmax(logits, axis=-1).astype(v.dtype)
        return jnp.einsum('bhqk,bqkd->bqhd', p, v_sel)

    out = lax.map(block, (to_blocks(q), to_blocks(qi), to_blocks(wi), to_blocks(pos),
                          key_idx.reshape(nQ, Q_BLOCK)))
    return out.swapaxes(0, 1).reshape(B_, T, H, dh)


def even_mixer(h, pos, rel_bias, w_in, ret_norm_g, dsa_q_g, dsa_k_g, w_out):
    B_, T, _ = h.shape
    qr, kr, vr, gr, qa, ka, va, qi, ki, wi = jnp.split(h @ w_in, EVEN_SPLITS, axis=-1)
    y_ret = retention(qr.reshape(B_, T, RET_HEADS, RET_DK), kr.reshape(B_, T, RET_HEADS, RET_DK),
                      vr.reshape(B_, T, RET_HEADS, RET_DV), pos)
    y_ret = rmsnorm(y_ret, ret_norm_g).reshape(B_, T, RET_HEADS * RET_DV) * jax.nn.silu(gr)
    y_att = dsa_attention(qa.reshape(B_, T, DSA_HEADS, DSA_DH), ka, va,
                          qi.reshape(B_, T, IDX_HEADS, IDX_DH), ki, wi, pos, rel_bias, dsa_q_g, dsa_k_g)
    y = jnp.concatenate([y_ret.astype(h.dtype), y_att.reshape(B_, T, DSA_HEADS * DSA_DH).astype(h.dtype)], axis=-1)
    return y @ w_out


def ssd_scan(x, dt, A, Bm, Cm):
    B_, T, H, P = x.shape
    G, N = Bm.shape[2], Bm.shape[3]
    R = H // G
    L = SSM_CHUNK
    nC = T // L
    xc = x.reshape(B_, nC, L, G, R, P)
    dtc = dt.reshape(B_, nC, L, G, R)
    Bc = Bm.reshape(B_, nC, L, G, N)
    Cc = Cm.reshape(B_, nC, L, G, N)
    cs = jnp.cumsum(dtc * A.reshape(G, R), axis=2)
    causal = jnp.tril(jnp.ones((L, L), dtype=bool))
    seg = cs[:, :, :, None] - cs[:, :, None]
    decay = jnp.exp(jnp.where(causal[:, :, None, None], seg, -jnp.inf))
    xdt = xc * dtc[..., None]
    cb = jnp.einsum('bclgn,bcsgn->bclsg', Cc, Bc)
    y_diag = jnp.einsum('bclsg,bclsgr,bcsgrp->bclgrp', cb, decay, xdt)
    decay_to_end = jnp.exp(cs[:, :, -1:] - cs)
    states = jnp.einsum('bclgn,bclgr,bclgrp->cbgrpn', Bc, decay_to_end, xdt)
    chunk_decay = jnp.exp(cs[:, :, -1]).swapaxes(0, 1)

    def step(s, inp):
        st, dec = inp
        return s * dec[..., None, None] + st, s

    _, prev = lax.scan(step, jnp.zeros_like(states[0]), (states, chunk_decay))
    y_off = jnp.einsum('bclgn,cbgrpn,bclgr->bclgrp', Cc, prev, jnp.exp(cs))
    return (y_diag + y_off).reshape(B_, T, H, P)


def ssd_mixer(h, w_in, conv_w, conv_b, dt_bias, a_log, d_skip, norm_g, w_out):
    B_, T, _ = h.shape
    z, xbc, dt = jnp.split(h @ w_in, [SSM_DINNER, SSM_DINNER + CONV_DIM], axis=-1)
    xbc = lax.conv_general_dilated(xbc, conv_w[:, None, :], window_strides=(1,),
                                   padding=[(SSM_CONV - 1, 0)],
                                   dimension_numbers=('NWC', 'WIO', 'NWC'),
                                   feature_group_count=CONV_DIM) + conv_b
    xbc = jax.nn.silu(xbc)
    xs, Bm, Cm = jnp.split(xbc, [SSM_DINNER, SSM_DINNER + SSM_GROUPS * SSM_STATE], axis=-1)
    xs = xs.reshape(B_, T, SSM_HEADS, SSM_HEADDIM)
    Bm = Bm.reshape(B_, T, SSM_GROUPS, SSM_STATE)
    Cm = Cm.reshape(B_, T, SSM_GROUPS, SSM_STATE)
    dt = jax.nn.softplus(dt.astype(jnp.float32) + dt_bias.astype(jnp.float32))
    A = -jnp.exp(a_log.astype(jnp.float32))
    y = ssd_scan(xs, dt, A, Bm, Cm) + xs * d_skip[:, None]
    yz = (y.reshape(B_, T, SSM_DINNER) * jax.nn.silu(z)).reshape(B_, T, SSM_GROUPS, SSM_DINNER // SSM_GROUPS)
    y = rmsnorm(yz, norm_g.reshape(SSM_GROUPS, SSM_DINNER // SSM_GROUPS)).reshape(B_, T, SSM_DINNER)
    return y.astype(h.dtype) @ w_out


def moe_ffn(h, router_w, router_b, w1, w3, w2, ws1, ws3, ws2):
    B_, T, D = h.shape
    N = B_ * T
    xt = h.reshape(N, D)
    scores = jax.nn.sigmoid((xt @ router_w).astype(jnp.float32))
    _, idx = lax.top_k(scores + router_b.astype(jnp.float32), TOPK)
    sel = jnp.take_along_axis(scores, idx, axis=1)
    gates = sel / jnp.sum(sel, axis=-1, keepdims=True) * ROUTED_SCALE
    A_ = N * TOPK
    flat_e = idx.reshape(A_)
    flat_tok = jnp.arange(A_, dtype=jnp.int32) // TOPK
    flat_g = gates.reshape(A_)
    order = jnp.argsort(flat_e)
    e_sorted = flat_e[order]
    counts = jnp.zeros((N_EXPERTS,), jnp.int32).at[flat_e].add(1)
    padded = (counts + MOE_BLOCK - 1) // MOE_BLOCK * MOE_BLOCK
    start = jnp.cumsum(counts) - counts
    pend = jnp.cumsum(padded)
    pstart = pend - padded
    dest = pstart[e_sorted] + (jnp.arange(A_, dtype=jnp.int32) - start[e_sorted])
    n_blocks = (A_ + N_EXPERTS * (MOE_BLOCK - 1) + MOE_BLOCK - 1) // MOE_BLOCK
    P_ = n_blocks * MOE_BLOCK
    slot_tok = jnp.full((P_,), N, jnp.int32).at[dest].set(flat_tok[order])
    slot_gate = jnp.zeros((P_,), jnp.float32).at[dest].set(flat_g[order])
    block_exp = jnp.minimum(jnp.searchsorted(pend, jnp.arange(n_blocks, dtype=jnp.int32) * MOE_BLOCK,
                                             side='right'), N_EXPERTS - 1)
    xpad = jnp.concatenate([xt, jnp.zeros((1, D), xt.dtype)], axis=0)
    xs = xpad[slot_tok].reshape(n_blocks, MOE_BLOCK, D)

    def expert_block(args):
        xb, e = args
        return (jax.nn.silu(xb @ w1[e]) * (xb @ w3[e])) @ w2[e]

    ys = lax.map(expert_block, (xs, block_exp)).reshape(P_, D)
    routed = jnp.zeros((N + 1, D), jnp.float32).at[slot_tok].add(
        ys.astype(jnp.float32) * slot_gate[:, None])[:N]
    shared = (jax.nn.silu(xt @ ws1) * (xt @ ws3)) @ ws2
    return (routed.astype(h.dtype) + shared).reshape(B_, T, D)


def setup_inputs(seed: int = 0) -> dict:
    key = jax.random.key(seed)
    ks = jax.random.split(key, 32)
    f32 = jnp.float32
    D = D_MODEL

    def nrm(k, shape, scale):
        return jax.random.normal(k, shape, f32) * scale

    x = nrm(ks[0], (BATCH, SEQ, D), 1.0)
    c = nrm(ks[1], (BATCH, D), 1.0)
    positions = (jax.random.randint(ks[2], (BATCH, 1), 0, 1024, dtype=jnp.int32)
                 + jnp.arange(SEQ, dtype=jnp.int32)[None, :]).astype(jnp.int32)
    ada_w = nrm(ks[3], (DEPTH, D, 6 * D), 0.5 * D ** -0.5)
    ada_b = nrm(ks[4], (DEPTH, 6 * D), 0.02)
    norm_mix_g = 1.0 + nrm(ks[5], (DEPTH, D), 0.02)
    norm_ffn_g = 1.0 + nrm(ks[6], (DEPTH, D), 0.02)
    rel_bias = nrm(ks[7], (REL_BUCKETS, DSA_HEADS), 0.5)
    even_w_in = nrm(ks[8], (N_EVEN, D, EVEN_IN), D ** -0.5)
    ret_norm_g = 1.0 + nrm(ks[9], (N_EVEN, RET_HEADS, RET_DV), 0.02)
    dsa_q_g = 1.0 + nrm(ks[10], (N_EVEN, DSA_DH), 0.02)
    dsa_k_g = 1.0 + nrm(ks[11], (N_EVEN, DSA_DH), 0.02)
    even_w_out = nrm(ks[12], (N_EVEN, MIX_OUT_EVEN, D), MIX_OUT_EVEN ** -0.5)
    ssm_w_in = nrm(ks[13], (N_ODD, D, SSM_IN), D ** -0.5)
    ssm_conv_w = nrm(ks[14], (N_ODD, SSM_CONV, CONV_DIM), SSM_CONV ** -0.5)
    ssm_conv_b = nrm(ks[15], (N_ODD, CONV_DIM), 0.02)
    dt0 = jnp.exp(jax.random.uniform(ks[16], (N_ODD, SSM_HEADS), f32, math.log(1e-3), math.log(1e-1)))
    ssm_dt_bias = dt0 + jnp.log(-jnp.expm1(-dt0))
    ssm_a_log = jnp.log(jax.random.uniform(ks[17], (N_ODD, SSM_HEADS), f32, 1.0, 16.0))
    ssm_d = 1.0 + nrm(ks[18], (N_ODD, SSM_HEADS), 0.02)
    ssm_norm_g = 1.0 + nrm(ks[19], (N_ODD, SSM_DINNER), 0.02)
    ssm_w_out = nrm(ks[20], (N_ODD, SSM_DINNER, D), SSM_DINNER ** -0.5)
    router_w = nrm(ks[21], (DEPTH, D, N_EXPERTS), D ** -0.5)
    router_b = nrm(ks[22], (DEPTH, N_EXPERTS), 0.01)
    exp_w1 = nrm(ks[23], (DEPTH, N_EXPERTS, D, D_EXPERT), D ** -0.5)
    exp_w3 = nrm(ks[24], (DEPTH, N_EXPERTS, D, D_EXPERT), D ** -0.5)
    exp_w2 = nrm(ks[25], (DEPTH, N_EXPERTS, D_EXPERT, D), D_EXPERT ** -0.5)
    sh_w1 = nrm(ks[26], (DEPTH, D, D_SHARED), D ** -0.5)
    sh_w3 = nrm(ks[27], (DEPTH, D, D_SHARED), D ** -0.5)
    sh_w2 = nrm(ks[28], (DEPTH, D_SHARED, D), D_SHARED ** -0.5)
    return {'x': x, 'c': c, 'positions': positions, 'ada_w': ada_w, 'ada_b': ada_b,
            'norm_mix_g': norm_mix_g, 'norm_ffn_g': norm_ffn_g, 'rel_bias': rel_bias,
            'even_w_in': even_w_in, 'ret_norm_g': ret_norm_g, 'dsa_q_g': dsa_q_g, 'dsa_k_g': dsa_k_g,
            'even_w_out': even_w_out, 'ssm_w_in': ssm_w_in, 'ssm_conv_w': ssm_conv_w,
            'ssm_conv_b': ssm_conv_b, 'ssm_dt_bias': ssm_dt_bias, 'ssm_a_log': ssm_a_log,
            'ssm_d': ssm_d, 'ssm_norm_g': ssm_norm_g, 'ssm_w_out': ssm_w_out,
            'router_w': router_w, 'router_b': router_b, 'exp_w1': exp_w1, 'exp_w3': exp_w3,
            'exp_w2': exp_w2, 'sh_w1': sh_w1, 'sh_w3': sh_w3, 'sh_w2': sh_w2}


def reference(x, c, positions, ada_w, ada_b, norm_mix_g, norm_ffn_g, rel_bias,
              even_w_in, ret_norm_g, dsa_q_g, dsa_k_g, even_w_out,
              ssm_w_in, ssm_conv_w, ssm_conv_b, ssm_dt_bias, ssm_a_log, ssm_d, ssm_norm_g, ssm_w_out,
              router_w, router_b, exp_w1, exp_w3, exp_w2, sh_w1, sh_w3, sh_w2):
    cond = jax.nn.silu(c)
    for i in range(DEPTH):
        mod = cond @ ada_w[i] + ada_b[i]
        sh1, sc1, g1, sh2, sc2, g2 = jnp.split(mod, 6, axis=-1)
        h = modulate(rmsnorm(x, norm_mix_g[i]), sh1, sc1)
        j = i // 2
        if i % 2 == 0:
            mix = even_mixer(h, positions, rel_bias, even_w_in[j], ret_norm_g[j],
                             dsa_q_g[j], dsa_k_g[j], even_w_out[j])
        else:
            mix = ssd_mixer(h, ssm_w_in[j], ssm_conv_w[j], ssm_conv_b[j], ssm_dt_bias[j],
                            ssm_a_log[j], ssm_d[j], ssm_norm_g[j], ssm_w_out[j])
        x = x + g1[:, None, :] * mix
        h = modulate(rmsnorm(x, norm_ffn_g[i]), sh2, sc2)
        x = x + g2[:, None, :] * moe_ffn(h, router_w[i], router_b[i], exp_w1[i], exp_w3[i], exp_w2[i],
                                         sh_w1[i], sh_w3[i], sh_w2[i])
    return x
```

```python
import functools
import math

import jax
import jax.numpy as jnp
from jax import lax
from jax.experimental import pallas as pl
from jax.experimental.pallas import tpu as pltpu

F32 = jnp.float32
BF16 = jnp.bfloat16
I32 = jnp.int32

EPS = 1e-6
RET_HEADS = 4
RET_DK = 128
RET_DV = 128
RET_CHUNK = 128
ROPE_THETA = 10000.0
DSA_HEADS = 8
DSA_DH = 64
IDX_HEADS = 8
IDX_DH = 32
DSA_TOPK_MAX = 256
REL_BUCKETS = 32
REL_MAX_DIST = 128
SSM_HEADDIM = 64
SSM_GROUPS = 8
SSM_STATE = 128
SSM_CONV = 4
SSM_CHUNK = 128
N_EXPERTS = 64
TOPK = 8
ROUTED_SCALE = 2.5

LANES = 128
VMEM_LIMIT = 48 * 1024 * 1024
NEG_BIG = -1e30
INT_MIN = -(2 ** 31)
M_NEGINF = -2139095041


def _cparams(sem):
    return pltpu.CompilerParams(dimension_semantics=sem, vmem_limit_bytes=VMEM_LIMIT)


def _silu(v):
    return v * (1.0 / (1.0 + jnp.exp(-v)))


def _sigmoid(v):
    return 1.0 / (1.0 + jnp.exp(-v))


def _mod_kernel(c_ref, w_ref, b_ref, o_ref):
    cond = _silu(c_ref[...]).astype(BF16)
    o_ref[...] = jnp.dot(cond, w_ref[...].astype(BF16), preferred_element_type=F32) + b_ref[...]


def _adaln_mod(c, w, b):
    bsz, d = c.shape
    n_out = w.shape[1]
    tn = 1024
    return pl.pallas_call(
        _mod_kernel,
        out_shape=jax.ShapeDtypeStruct((bsz, n_out), F32),
        grid=(n_out // tn,),
        in_specs=[pl.BlockSpec((bsz, d), lambda j: (0, 0)),
                  pl.BlockSpec((d, tn), lambda j: (0, j)),
                  pl.BlockSpec((1, tn), lambda j: (0, j))],
        out_specs=pl.BlockSpec((bsz, tn), lambda j: (0, j)),
        compiler_params=_cparams(("arbitrary",)),
        name="adaln_mod",
    )(c, w, b.reshape(1, n_out))


def _norm_mod(x, g, sc, sh):
    ms = jnp.mean(x * x, axis=-1, keepdims=True)
    return (x * lax.rsqrt(ms + EPS) * g) * (1.0 + sc) + sh


def _inproj_even_kernel(x_ref, sc_ref, sh_ref, g_ref, w_ref, kg_ref,
                        ret_ref, qa_ref, kn_ref, va_ref, qi_ref, kw_ref):
    h = _norm_mod(x_ref[...], g_ref[...], sc_ref[0], sh_ref[0])
    p = jnp.dot(h.astype(BF16), w_ref[...], preferred_element_type=F32)
    ret_ref[...] = p[:, :2048].astype(BF16)
    qa_ref[...] = p[:, 2048:2560].astype(BF16)
    kv = p[:, 2560:2688]
    ka = kv[:, :DSA_DH]
    kn = ka * lax.rsqrt(jnp.mean(ka * ka, axis=-1, keepdims=True) + EPS) * kg_ref[...]
    kn_ref[...] = kn.astype(BF16)
    lane = lax.broadcasted_iota(I32, kv.shape, 1)
    rolled = pltpu.roll(kv, DSA_DH, axis=1)
    vaug = jnp.where(lane < DSA_DH, rolled, jnp.where(lane == DSA_DH, 1.0, 0.0))
    va_ref[...] = vaug.astype(BF16)
    qi_ref[...] = p[:, 2688:2944].astype(BF16)
    kw_ref[...] = p[:, 2944:3072]


def _inproj_even(x2, sc, sh, g, w_pad, k_gain, seq, tm=512):
    n, d = x2.shape
    tpb = seq // tm
    cols = w_pad.shape[1]
    bmap = lambda i: (i // tpb, 0, 0)
    row = lambda i: (i, 0)
    const = lambda i: (0, 0)
    outs = (jax.ShapeDtypeStruct((n, 2048), BF16), jax.ShapeDtypeStruct((n, 512), BF16),
            jax.ShapeDtypeStruct((n, DSA_DH), BF16), jax.ShapeDtypeStruct((n, LANES), BF16),
            jax.ShapeDtypeStruct((n, 256), BF16), jax.ShapeDtypeStruct((n, LANES), F32))
    return pl.pallas_call(
        _inproj_even_kernel,
        out_shape=outs,
        grid=(n // tm,),
        in_specs=[pl.BlockSpec((tm, d), row),
                  pl.BlockSpec((1, 1, d), bmap), pl.BlockSpec((1, 1, d), bmap),
                  pl.BlockSpec((1, d), const),
                  pl.BlockSpec((d, cols), const),
                  pl.BlockSpec((1, DSA_DH), const)],
        out_specs=(pl.BlockSpec((tm, 2048), row), pl.BlockSpec((tm, 512), row),
                   pl.BlockSpec((tm, DSA_DH), row), pl.BlockSpec((tm, LANES), row),
                   pl.BlockSpec((tm, 256), row), pl.BlockSpec((tm, LANES), row)),
        compiler_params=_cparams(("arbitrary",)),
        name="inproj_even",
    )(x2, sc, sh, g.reshape(1, d), w_pad, k_gain.reshape(1, DSA_DH))


def _retention_kernel(in_ref, pos_ref, inv_ref, sgn_ref, dintra_ref, xi_ref, zeta_ref, cd_ref, g_ref,
                      o_ref, r_ref):
    c = pl.program_id(1)

    @pl.when(c == 0)
    def _():
        r_ref[...] = jnp.zeros_like(r_ref)

    blk = in_ref[0]
    posf = pos_ref[0].astype(F32)
    ang = posf * inv_ref[...]
    cos2 = jnp.cos(ang)
    sin2 = jnp.sin(ang) * sgn_ref[...]
    half = RET_DK // 2
    outs = []
    for h in range(RET_HEADS):
        q = blk[:, h * RET_DK:(h + 1) * RET_DK].astype(F32)
        k = blk[:, 512 + h * RET_DK:512 + (h + 1) * RET_DK].astype(F32)
        v = blk[:, 1024 + h * RET_DV:1024 + (h + 1) * RET_DV]
        gate = blk[:, 1536 + h * RET_DV:1536 + (h + 1) * RET_DV].astype(F32)
        q = q * cos2 + pltpu.roll(q, half, axis=1) * sin2
        k = (k * cos2 + pltpu.roll(k, half, axis=1) * sin2) * (RET_DK ** -0.5)
        qb = q.astype(BF16)
        kb = k.astype(BF16)
        inner = lax.dot_general(qb, kb, (((1,), (1,)), ((), ())), preferred_element_type=F32)
        inner = inner * dintra_ref[h]
        y = jnp.dot(inner.astype(BF16), v, preferred_element_type=F32)
        r_prev = r_ref[h]
        y = y + jnp.dot((q * xi_ref[h]).astype(BF16), r_prev.astype(BF16), preferred_element_type=F32)
        kz = (k * zeta_ref[h]).astype(BF16)
        u = lax.dot_general(kz, v, (((0,), (0,)), ((), ())), preferred_element_type=F32)
        r_ref[h] = r_prev * cd_ref[h] + u
        yn = y * lax.rsqrt(jnp.mean(y * y, axis=-1, keepdims=True) + EPS) * g_ref[h]
        outs.append((yn * _silu(gate)).astype(BF16))
    o_ref[0] = jnp.concatenate(outs, axis=1)


def _retention(ret_in, pos3, ret_norm_g):
    bsz, seq, _ = ret_in.shape
    L = RET_CHUNK
    nc = seq // L
    H = RET_HEADS
    half = RET_DK // 2
    inv = ROPE_THETA ** (-jnp.arange(half, dtype=F32) / half)
    inv2 = jnp.concatenate([inv, inv]).reshape(1, RET_DK)
    sgn = jnp.concatenate([-jnp.ones((half,), F32), jnp.ones((half,), F32)]).reshape(1, RET_DK)
    log_g = jnp.log(1.0 - 2.0 ** (-5.0 - jnp.arange(H, dtype=F32)))
    idx = jnp.arange(L, dtype=F32)
    rel = idx[:, None] - idx[None, :]
    d_intra = jnp.where(rel >= 0, jnp.exp(log_g[:, None, None] * jnp.maximum(rel, 0.0)), 0.0)
    zeta = jnp.exp(log_g[:, None] * (L - 1 - idx))
    xi = jnp.exp(log_g[:, None] * (idx + 1))
    cd = jnp.exp(log_g * L)
    xi_b = jnp.broadcast_to(xi[:, :, None], (H, L, RET_DK))
    zeta_b = jnp.broadcast_to(zeta[:, :, None], (H, L, RET_DK))
    cd_b = jnp.broadcast_to(cd[:, None, None], (H, 1, RET_DV))
    c3 = lambda b, c: (0, 0, 0)
    c2 = lambda b, c: (0, 0)
    return pl.pallas_call(
        _retention_kernel,
        out_shape=jax.ShapeDtypeStruct((bsz, seq, H * RET_DV), BF16),
        grid=(bsz, nc),
        in_specs=[pl.BlockSpec((1, L, 2048), lambda b, c: (b, c, 0)),
                  pl.BlockSpec((1, L, 1), lambda b, c: (b, c, 0)),
                  pl.BlockSpec((1, RET_DK), c2), pl.BlockSpec((1, RET_DK), c2),
                  pl.BlockSpec((H, L, L), c3), pl.BlockSpec((H, L, RET_DK), c3),
                  pl.BlockSpec((H, L, RET_DK), c3), pl.BlockSpec((H, 1, RET_DV), c3),
                  pl.BlockSpec((H, 1, RET_DV), c3)],
        out_specs=pl.BlockSpec((1, L, H * RET_DV), lambda b, c: (b, c, 0)),
        scratch_shapes=[pltpu.VMEM((H, RET_DK, RET_DV), F32)],
        compiler_params=_cparams(("arbitrary", "arbitrary")),
        name="retention",
    )(ret_in, pos3, inv2, sgn, d_intra, xi_b, zeta_b, cd_b, ret_norm_g.reshape(H, 1, RET_DV))


def _dsa_kernel(minq_ref, maxk_ref,
                qa_ref, qi_ref, kwq_ref, posq_ref, kit_ref, kt_ref, va_ref, posk_ref,
                qg_ref, btab_ref, bfar_ref,
                o_ref,
                qs_ref, qis_ref, wb_ref, m_ref, lg_ref, p_ref, al_ref, mrun_ref, acc_ref,
                *, qb_size, topk, idx_bits):
    QB = qb_size
    H = DSA_HEADS
    b = pl.program_id(0)
    qb = pl.program_id(1)
    nq = pl.num_programs(1)
    nk = qb + 1
    q0 = qb * QB

    qa = qa_ref[0].astype(F32)
    qi = qi_ref[0]
    kwq = kwq_ref[0]
    for h in range(H):
        qh = qa[:, h * DSA_DH:(h + 1) * DSA_DH]
        qh = qh * lax.rsqrt(jnp.mean(qh * qh, axis=-1, keepdims=True) + EPS) * qg_ref[...]
        qs_ref[h * QB:(h + 1) * QB, :] = (qh * (DSA_DH ** -0.5)).astype(BF16)
        qis_ref[h * QB:(h + 1) * QB, :] = qi[:, h * IDX_DH:(h + 1) * IDX_DH]
        w = kwq[:, IDX_DH + h:IDX_DH + h + 1] * (IDX_HEADS ** -0.5 * IDX_DH ** -0.5)
        wb_ref[h] = jnp.broadcast_to(w, (QB, LANES))

    nrep = QB // LANES

    def rep(t):
        return jnp.concatenate([t] * nrep, axis=1)

    qidx = q0 + lax.broadcasted_iota(I32, (QB, QB), 0)
    lane_k = lax.broadcasted_iota(I32, (QB, QB), 1)

    def score_chunk(kc, carry):
        k0 = pl.multiple_of(kc * QB, QB)
        a = jnp.dot(qis_ref[...], kit_ref[0, :, pl.ds(k0, QB)], preferred_element_type=F32)
        s = jnp.zeros((QB, QB), F32)
        for h in range(H):
            s = s + jnp.maximum(a[h * QB:(h + 1) * QB], 0.0) * rep(wb_ref[h])
        s = jnp.where(k0 + lane_k <= qidx, s, -jnp.inf)
        bits = pltpu.bitcast(s, I32)
        m_ref[:, pl.ds(k0, QB)] = jnp.where(bits < 0, bits ^ 0x7FFFFFFF, bits)
        return carry

    lax.fori_loop(0, nk, score_chunk, 0)

    ones_k = jnp.ones((QB, LANES), BF16)

    def count(pred):
        def body(kc, cnt):
            k0 = pl.multiple_of(kc * QB, QB)
            msk = pred(m_ref[:, pl.ds(k0, QB)], k0 + lane_k)
            return cnt + jnp.dot(jnp.where(msk, 1.0, 0.0).astype(BF16), ones_k, preferred_element_type=F32)
        return lax.fori_loop(0, nk, body, jnp.zeros((QB, LANES), F32))

    kf = float(topk)

    def bit_body(i, ans):
        cand = ans + jnp.left_shift(jnp.int32(1), 31 - i)
        cand_r = rep(cand)
        cnt = count(lambda m, kidx: m >= cand_r)
        return jnp.where(cnt >= kf, cand, ans)

    ans = lax.fori_loop(0, 32, bit_body, jnp.full((QB, LANES), INT_MIN, I32))
    ans_r = rep(ans)

    ans1_r = rep(ans + 1)
    c_gt = count(lambda m, kidx: m >= ans1_r)
    c_ge = count(lambda m, kidx: m >= ans_r)
    need = kf - c_gt
    tie = jnp.logical_and(c_ge > kf, ans > M_NEGINF)
    any_tie = jnp.max(jnp.where(tie, 1, 0))

    def tie_search():
        def jbody(i, jc):
            cand = jc + jnp.left_shift(jnp.int32(1), idx_bits - 1 - i)
            cand_r = rep(cand)
            c = count(lambda m, kidx: jnp.logical_and(m == ans_r, kidx < cand_r))
            return jnp.where(c <= need, cand, jc)
        return lax.fori_loop(0, idx_bits, jbody, jnp.zeros((QB, LANES), I32))

    jcut = lax.cond(any_tie > 0, tie_search, lambda: jnp.full((QB, LANES), 2 ** 30, I32))
    jcut_r = rep(jcut)

    mrun_ref[...] = jnp.full(mrun_ref.shape, NEG_BIG, F32)
    acc_ref[...] = jnp.zeros_like(acc_ref)
    posq = posq_ref[0]
    min_q = minq_ref[b * nq + qb]

    def att_chunk(kc, carry):
        k0 = pl.multiple_of(kc * QB, QB)
        m = m_ref[:, pl.ds(k0, QB)]
        kidx = k0 + lane_k
        sel = jnp.logical_or(m > ans_r, jnp.logical_and(m == ans_r, kidx < jcut_r))
        sel = jnp.logical_and(sel, kidx <= qidx)
        lg = jnp.dot(qs_ref[...], kt_ref[0, :, pl.ds(k0, QB)], preferred_element_type=F32)
        far = (min_q - maxk_ref[b * nq + kc]) >= REL_MAX_DIST

        @pl.when(far)
        def _():
            for h in range(H):
                bias = rep(jnp.broadcast_to(bfar_ref[h:h + 1, :], (QB, LANES)))
                lg_ref[h * QB:(h + 1) * QB, :] = jnp.where(sel, lg[h * QB:(h + 1) * QB] + bias, NEG_BIG)

        @pl.when(jnp.logical_not(far))
        def _():
            n = jnp.maximum(posq - posk_ref[0, :, pl.ds(k0, QB)], 0)
            is_far = n >= REL_MAX_DIST
            ncl = jnp.minimum(n, REL_MAX_DIST - 1)
            for h in range(H):
                tbl = jnp.broadcast_to(btab_ref[h:h + 1, :], (QB, LANES))
                near = jnp.concatenate(
                    [jnp.take_along_axis(tbl, ncl[:, j * LANES:(j + 1) * LANES], axis=1) for j in range(nrep)],
                    axis=1)
                bias = jnp.where(is_far, rep(jnp.broadcast_to(bfar_ref[h:h + 1, :], (QB, LANES))), near)
                lg_ref[h * QB:(h + 1) * QB, :] = jnp.where(sel, lg[h * QB:(h + 1) * QB] + bias, NEG_BIG)

        for h in range(H):
            rows = slice(h * QB, (h + 1) * QB)
            lgh = lg_ref[rows, :]
            m_old = mrun_ref[rows, :]
            m_new = jnp.maximum(m_old, jnp.max(lgh, axis=1, keepdims=True))
            al_ref[rows, :] = jnp.exp(m_old - m_new)
            mrun_ref[rows, :] = m_new
            p_ref[rows, :] = jnp.exp(lgh - rep(m_new)).astype(BF16)
        pv = jnp.dot(p_ref[...], va_ref[0, pl.ds(k0, QB), :], preferred_element_type=F32)
        acc_ref[...] = al_ref[...] * acc_ref[...] + pv
        return carry

    lax.fori_loop(0, nk, att_chunk, 0)

    outs = []
    for h in range(H):
        a = acc_ref[h * QB:(h + 1) * QB, :]
        outs.append(a[:, :DSA_DH] / a[:, DSA_DH:DSA_DH + 1])
    o_ref[0] = jnp.concatenate(outs, axis=1).astype(BF16)


def _t5_bucket(n):
    max_exact = REL_BUCKETS // 2
    nf = jnp.maximum(n, 1).astype(F32)
    large = max_exact + (jnp.log(nf / max_exact) / math.log(REL_MAX_DIST / max_exact)
                         * (REL_BUCKETS - max_exact)).astype(I32)
    large = jnp.minimum(large, REL_BUCKETS - 1)
    return jnp.where(n < max_exact, n, large)


def _dsa(qa, qi, kw, kn, vaug, positions, rel_bias, q_gain, qb_size=256):
    bsz, seq, _ = qa.shape
    QB = min(qb_size, seq)
    nq = seq // QB
    H = DSA_HEADS
    topk = min(DSA_TOPK_MAX, seq // 4)
    idx_bits = int(math.log2(seq)) + 1
    kt = jnp.swapaxes(kn, 1, 2)
    kit = jnp.swapaxes(kw[:, :, :IDX_DH].astype(BF16), 1, 2)
    posq = positions.reshape(bsz, seq, 1)
    posk = positions.reshape(bsz, 1, seq)
    pblk = positions.reshape(bsz, nq, QB)
    minq = jnp.min(pblk, axis=-1).reshape(-1)
    maxk = jnp.max(pblk, axis=-1).reshape(-1)
    btab = rel_bias[_t5_bucket(jnp.arange(REL_MAX_DIST, dtype=I32))].T
    bfar = jnp.broadcast_to(rel_bias[REL_BUCKETS - 1][:, None], (H, LANES))
    blk_q = lambda b, q, *_: (b, q, 0)
    per_b = lambda b, q, *_: (b, 0, 0)
    const = lambda b, q, *_: (0, 0)
    kern = functools.partial(_dsa_kernel, qb_size=QB, topk=topk, idx_bits=idx_bits)
    grid_spec = pltpu.PrefetchScalarGridSpec(
        num_scalar_prefetch=2,
        grid=(bsz, nq),
        in_specs=[pl.BlockSpec((1, QB, 512), blk_q), pl.BlockSpec((1, QB, 256), blk_q),
                  pl.BlockSpec((1, QB, LANES), blk_q), pl.BlockSpec((1, QB, 1), blk_q),
                  pl.BlockSpec((1, IDX_DH, seq), per_b), pl.BlockSpec((1, DSA_DH, seq), per_b),
                  pl.BlockSpec((1, seq, LANES), per_b), pl.BlockSpec((1, 1, seq), per_b),
                  pl.BlockSpec((1, DSA_DH), const), pl.BlockSpec((H, LANES), const),
                  pl.BlockSpec((H, LANES), const)],
        out_specs=pl.BlockSpec((1, QB, H * DSA_DH), blk_q),
        scratch_shapes=[pltpu.VMEM((H * QB, DSA_DH), BF16), pltpu.VMEM((H * QB, IDX_DH), BF16),
                        pltpu.VMEM((H, QB, LANES), F32), pltpu.VMEM((QB, seq), I32),
                        pltpu.VMEM((H * QB, QB), F32), pltpu.VMEM((H * QB, QB), BF16),
                        pltpu.VMEM((H * QB, LANES), F32), pltpu.VMEM((H * QB, LANES), F32),
                        pltpu.VMEM((H * QB, LANES), F32)])
    return pl.pallas_call(
        kern,
        out_shape=jax.ShapeDtypeStruct((bsz, seq, H * DSA_DH), BF16),
        grid_spec=grid_spec,
        compiler_params=_cparams(("arbitrary", "arbitrary")),
        name="dsa",
    )(minq, maxk, qa, qi, kw, posq, kit, kt, vaug, posk, q_gain.reshape(1, DSA_DH), btab, bfar)


def _inproj_ssm_kernel(x_ref, sc_ref, sh_ref, g_ref, w_ref, wdt_ref, zx_ref, dt_ref, h_ref):
    j = pl.program_id(1)

    @pl.when(j == 0)
    def _():
        h = _norm_mod(x_ref[...], g_ref[...], sc_ref[0], sh_ref[0]).astype(BF16)
        h_ref[...] = h
        dt_ref[...] = jnp.dot(h, wdt_ref[...], preferred_element_type=F32)

    zx_ref[...] = jnp.dot(h_ref[...], w_ref[...], preferred_element_type=F32).astype(BF16)


def _inproj_ssm(x2, sc, sh, g, w_main, w_dt, seq, tm=512, tn=2048):
    n, d = x2.shape
    tpb = seq // tm
    cols = w_main.shape[1]
    bmap = lambda i, j: (i // tpb, 0, 0)
    return pl.pallas_call(
        _inproj_ssm_kernel,
        out_shape=(jax.ShapeDtypeStruct((n, cols), BF16), jax.ShapeDtypeStruct((n, LANES), F32)),
        grid=(n // tm, cols // tn),
        in_specs=[pl.BlockSpec((tm, d), lambda i, j: (i, 0)),
                  pl.BlockSpec((1, 1, d), bmap), pl.BlockSpec((1, 1, d), bmap),
                  pl.BlockSpec((1, d), lambda i, j: (0, 0)),
                  pl.BlockSpec((d, tn), lambda i, j: (0, j)),
                  pl.BlockSpec((d, LANES), lambda i, j: (0, 0))],
        out_specs=(pl.BlockSpec((tm, tn), lambda i, j: (i, j)), pl.BlockSpec((tm, LANES), lambda i, j: (i, 0))),
        scratch_shapes=[pltpu.VMEM((tm, d), BF16)],
        compiler_params=_cparams(("arbitrary", "arbitrary")),
        name="inproj_ssm",
    )(x2, sc, sh, g.reshape(1, d), w_main, w_dt)


def _split3(a):
    hi = a.astype(BF16)
    r1 = a - hi.astype(F32)
    mid = r1.astype(BF16)
    lo = (r1 - mid.astype(F32)).astype(BF16)
    return hi, mid, lo


def _ssd_kernel(zx_ref, dt_ref, dtt_ref, cw_ref, cb_ref, dtb_ref, dtbt_ref, a_ref, at_ref, dsk_ref, ng_ref,
                o_ref, state_ref, tail_ref, *, d_inner, n_heads):
    L = SSM_CHUNK
    P = SSM_HEADDIM
    NS = SSM_STATE
    G = SSM_GROUPS
    R = n_heads // G
    gn = G * NS
    c = pl.program_id(1)

    @pl.when(c == 0)
    def _():
        state_ref[...] = jnp.zeros_like(state_ref)
        tail_ref[...] = jnp.zeros_like(tail_ref)

    blk = zx_ref[0]
    z = blk[:, :d_inner].astype(F32)
    xin = blk[:, d_inner:].astype(F32)
    tail = tail_ref[...]
    row8 = lax.broadcasted_iota(I32, (8, xin.shape[1]), 0)
    conv = cb_ref[...] + cw_ref[SSM_CONV - 1:SSM_CONV, :] * xin
    for s in range(1, SSM_CONV):
        rolled = pltpu.roll(xin, s, axis=0)
        top = jnp.where(row8 < s, pltpu.roll(tail, s, axis=0), rolled[:8])
        shifted = jnp.concatenate([top, rolled[8:]], axis=0)
        conv = conv + cw_ref[SSM_CONV - 1 - s:SSM_CONV - s, :] * shifted
    tail_ref[...] = xin[L - 8:, :]
    xbc = _silu(conv)
    xs = xbc[:, :d_inner]
    bm = xbc[:, d_inner:d_inner + gn]
    cm = xbc[:, d_inner + gn:]

    def softplus(v):
        return jnp.maximum(v, 0.0) + jnp.log(1.0 + jnp.exp(-jnp.abs(v)))

    dt = softplus(dt_ref[0][:, :n_heads] + dtb_ref[...])
    dtt = softplus(dtt_ref[0] + dtbt_ref[...])
    da = dt * (-jnp.exp(a_ref[...]))
    dat = dtt * (-jnp.exp(at_ref[...]))
    r_i = lax.broadcasted_iota(I32, (L, L), 0)
    c_i = lax.broadcasted_iota(I32, (L, L), 1)
    causal = c_i <= r_i
    tril = jnp.where(causal, 1.0, 0.0).astype(BF16)
    triu = jnp.where(r_i <= c_i, 1.0, 0.0).astype(BF16)
    cs = sum(jnp.dot(tril, t, preferred_element_type=F32) for t in _split3(da))
    cst = sum(jnp.dot(t, triu, preferred_element_type=F32) for t in _split3(dat))
    cs_last = cs[L - 1:L, :]
    e_cs = jnp.exp(cs)
    e_end = jnp.exp(cs_last - cs)
    e_chunk = jnp.exp(cs_last)

    ys = []
    for g in range(G):
        bg = bm[:, g * NS:(g + 1) * NS].astype(BF16)
        cg = cm[:, g * NS:(g + 1) * NS].astype(BF16)
        cbm = lax.dot_general(cg, bg, (((1,), (1,)), ((), ())), preferred_element_type=F32)
        for r in range(R):
            h = g * R + r
            xh = xs[:, h * P:(h + 1) * P]
            xdt = xh * dt[:, h:h + 1]
            seg = cs[:, h:h + 1] - cst[h:h + 1, :]
            decay = jnp.exp(jnp.where(causal, seg, -jnp.inf))
            y = jnp.dot((cbm * decay).astype(BF16), xdt.astype(BF16), preferred_element_type=F32)
            st = state_ref[h]
            y_off = lax.dot_general(cg, st.astype(BF16), (((1,), (1,)), ((), ())), preferred_element_type=F32)
            y = y + y_off * e_cs[:, h:h + 1]
            upd = lax.dot_general((xdt * e_end[:, h:h + 1]).astype(BF16), bg, (((0,), (0,)), ((), ())),
                                  preferred_element_type=F32)
            state_ref[h] = st * e_chunk[:, h:h + 1] + upd
            ys.append(y + xh * dsk_ref[:, h:h + 1])
    y = jnp.concatenate(ys, axis=1) * _silu(z)
    gw = d_inner // G
    outs = []
    for g in range(G):
        yg = y[:, g * gw:(g + 1) * gw]
        outs.append(yg * lax.rsqrt(jnp.mean(yg * yg, axis=-1, keepdims=True) + EPS))
    o_ref[0] = (jnp.concatenate(outs, axis=1) * ng_ref[...]).astype(BF16)


def _ssd(zx, dt_raw, conv_w, conv_b, dt_bias, a_log, d_skip, norm_g, d_inner):
    bsz, seq, width = zx.shape
    L = SSM_CHUNK
    nc = seq // L
    H = d_inner // SSM_HEADDIM
    conv_dim = width - d_inner
    dtt = jnp.swapaxes(dt_raw[:, :, :H], 1, 2)
    c2 = lambda b, c: (0, 0)
    kern = functools.partial(_ssd_kernel, d_inner=d_inner, n_heads=H)
    return pl.pallas_call(
        kern,
        out_shape=jax.ShapeDtypeStruct((bsz, seq, d_inner), BF16),
        grid=(bsz, nc),
        in_specs=[pl.BlockSpec((1, L, width), lambda b, c: (b, c, 0)),
                  pl.BlockSpec((1, L, LANES), lambda b, c: (b, c, 0)),
                  pl.BlockSpec((1, H, L), lambda b, c: (b, 0, c)),
                  pl.BlockSpec((SSM_CONV, conv_dim), c2), pl.BlockSpec((1, conv_dim), c2),
                  pl.BlockSpec((1, H), c2), pl.BlockSpec((H, 1), c2),
                  pl.BlockSpec((1, H), c2), pl.BlockSpec((H, 1), c2),
                  pl.BlockSpec((1, H), c2), pl.BlockSpec((1, d_inner), c2)],
        out_specs=pl.BlockSpec((1, L, d_inner), lambda b, c: (b, c, 0)),
        scratch_shapes=[pltpu.VMEM((H, SSM_HEADDIM, SSM_STATE), F32), pltpu.VMEM((8, conv_dim), F32)],
        compiler_params=_cparams(("arbitrary", "arbitrary")),
        name="ssd",
    )(zx, dt_raw, dtt, conv_w, conv_b.reshape(1, conv_dim), dt_bias.reshape(1, H), dt_bias.reshape(H, 1),
      a_log.reshape(1, H), a_log.reshape(H, 1), d_skip.reshape(1, H), norm_g.reshape(1, d_inner))


def _outproj_route_kernel(*refs, n_y):
    x_ref = refs[0]
    y_refs = refs[1:1 + n_y]
    w_refs = refs[1 + n_y:1 + 2 * n_y]
    (g1_ref, sc_ref, sh_ref, gn_ref, rw_ref, rb_ref,
     x1_ref, h2_ref, eidx_ref, gate_ref, rank_ref, cnt_ref, carry_ref) = refs[1 + 2 * n_y:]
    i = pl.program_id(0)

    @pl.when(i == 0)
    def _():
        carry_ref[...] = jnp.zeros_like(carry_ref)

    acc = jnp.dot(y_refs[0][...], w_refs[0][...], preferred_element_type=F32)
    for yr, wr in zip(y_refs[1:], w_refs[1:]):
        acc = acc + jnp.dot(yr[...], wr[...], preferred_element_type=F32)
    x1 = x_ref[...] + g1_ref[0] * acc
    x1_ref[...] = x1
    h2 = _norm_mod(x1, gn_ref[...], sc_ref[0], sh_ref[0])
    h2b = h2.astype(BF16)
    h2_ref[...] = h2b

    tm = x1.shape[0]
    logits = jnp.dot(h2b, rw_ref[...], preferred_element_type=F32)
    scores = _sigmoid(logits)
    lane = lax.broadcasted_iota(I32, (tm, LANES), 1)
    lane_f = lane.astype(F32)
    vals = jnp.where(lane < N_EXPERTS, scores + rb_ref[...], -jnp.inf)
    eidx = jnp.zeros((tm, LANES), I32)
    gsel = jnp.zeros((tm, LANES), F32)
    oh_all = jnp.zeros((tm, LANES), F32)
    picks = []
    for k in range(TOPK):
        mx = jnp.max(vals, axis=1, keepdims=True)
        idx_f = jnp.min(jnp.where(vals == mx, lane_f, float(LANES)), axis=1, keepdims=True)
        onehot = lane_f == idx_f
        s_k = jnp.sum(jnp.where(onehot, scores, 0.0), axis=1, keepdims=True)
        vals = jnp.where(onehot, -jnp.inf, vals)
        eidx = jnp.where(lane == k, idx_f.astype(I32), eidx)
        gsel = jnp.where(lane == k, s_k, gsel)
        oh_all = oh_all + jnp.where(onehot, 1.0, 0.0)
        picks.append(onehot)
    denom = jnp.sum(gsel, axis=1, keepdims=True)
    gate_ref[...] = gsel / denom * ROUTED_SCALE
    eidx_ref[...] = eidx
    r_i = lax.broadcasted_iota(I32, (tm, tm), 0)
    c_i = lax.broadcasted_iota(I32, (tm, tm), 1)
    tri = jnp.where(c_i < r_i, 1.0, 0.0).astype(BF16)
    base = carry_ref[...] + jnp.dot(tri, oh_all.astype(BF16), preferred_element_type=F32)
    rank = jnp.zeros((tm, LANES), F32)
    for k in range(TOPK):
        r_k = jnp.sum(jnp.where(picks[k], base, 0.0), axis=1, keepdims=True)
        rank = jnp.where(lane == k, r_k, rank)
    rank_ref[...] = rank.astype(I32)
    carry = carry_ref[...] + jnp.sum(oh_all, axis=0, keepdims=True)
    carry_ref[...] = carry
    cnt_ref[...] = carry


def _outproj_route(x2, ys, ws, g1, sc2, sh2, gn, rw_pad, rb_pad, seq, tm=512):
    n, d = x2.shape
    tpb = seq // tm
    n_y = len(ys)
    bmap = lambda i: (i // tpb, 0, 0)
    row = lambda i: (i, 0)
    const = lambda i: (0, 0)
    in_specs = [pl.BlockSpec((tm, d), row)]
    in_specs += [pl.BlockSpec((tm, y.shape[1]), row) for y in ys]
    in_specs += [pl.BlockSpec(w.shape, const) for w in ws]
    in_specs += [pl.BlockSpec((1, 1, d), bmap)] * 3
    in_specs += [pl.BlockSpec((1, d), const), pl.BlockSpec((d, LANES), const), pl.BlockSpec((1, LANES), const)]
    outs = (jax.ShapeDtypeStruct((n, d), F32), jax.ShapeDtypeStruct((n, d), BF16),
            jax.ShapeDtypeStruct((n, LANES), I32), jax.ShapeDtypeStruct((n, LANES), F32),
            jax.ShapeDtypeStruct((n, LANES), I32), jax.ShapeDtypeStruct((1, LANES), F32))
    return pl.pallas_call(
        functools.partial(_outproj_route_kernel, n_y=n_y),
        out_shape=outs,
        grid=(n // tm,),
        in_specs=in_specs,
        out_specs=(pl.BlockSpec((tm, d), row), pl.BlockSpec((tm, d), row), pl.BlockSpec((tm, LANES), row),
                   pl.BlockSpec((tm, LANES), row), pl.BlockSpec((tm, LANES), row), pl.BlockSpec((1, LANES), const)),
        scratch_shapes=[pltpu.VMEM((1, LANES), F32)],
        compiler_params=_cparams(("arbitrary",)),
        name="outproj_route",
    )(x2, *ys, *ws, g1, sc2, sh2, gn.reshape(1, d), rw_pad, rb_pad)


MOE_ROWS = 256


def _dispatch_kernel(pstart_ref, cnt_ref, pad_ref, e_ref, r_ref, h_ref, z_ref, xs_ref, sem, zsem, *, tm):
    i = pl.program_id(0)

    @pl.when(i == 0)
    def _():
        def per_expert(e, tot):
            lo = pstart_ref[e] + cnt_ref[e]
            npad = pad_ref[e] - cnt_ref[e]

            def one(j, c):
                pltpu.make_async_copy(z_ref, xs_ref.at[lo + j], zsem).start()
                return c
            lax.fori_loop(0, npad, one, 0)
            return tot + npad
        total = lax.fori_loop(0, N_EXPERTS, per_expert, 0)

        def wait_one(j, c):
            pltpu.make_async_copy(z_ref, xs_ref.at[0], zsem).wait()
            return c
        lax.fori_loop(0, total, wait_one, 0)

    def per_tok(t, c):
        src = h_ref.at[i * tm + t]
        for k in range(TOPK):
            a = t * TOPK + k
            dst = pstart_ref[e_ref[a]] + r_ref[a]
            pltpu.make_async_copy(src, xs_ref.at[dst], sem).start()
        return c
    lax.fori_loop(0, tm, per_tok, 0)

    def wait_tok(t, c):
        for k in range(TOPK):
            pltpu.make_async_copy(h_ref.at[0], xs_ref.at[0], sem).wait()
        return c
    lax.fori_loop(0, tm, wait_tok, 0)


def _dispatch(h3, e_flat, r_flat, pstart, counts, padded, n_slots, tm=512):
    n = h3.shape[0]
    zrow = jnp.zeros(h3.shape[1:], h3.dtype)
    smem_blk = pl.BlockSpec((tm * TOPK,), lambda i, *_: (i,), memory_space=pltpu.SMEM)
    grid_spec = pltpu.PrefetchScalarGridSpec(
        num_scalar_prefetch=3,
        grid=(n // tm,),
        in_specs=[smem_blk, smem_blk,
                  pl.BlockSpec(memory_space=pl.ANY),
                  pl.BlockSpec(zrow.shape, lambda i, *_: (0, 0))],
        out_specs=pl.BlockSpec(memory_space=pl.ANY),
        scratch_shapes=[pltpu.SemaphoreType.DMA(()), pltpu.SemaphoreType.DMA(())])
    return pl.pallas_call(
        functools.partial(_dispatch_kernel, tm=tm),
        out_shape=jax.ShapeDtypeStruct((n_slots,) + h3.shape[1:], h3.dtype),
        grid_spec=grid_spec,
        compiler_params=_cparams(("arbitrary",)),
        name="moe_dispatch",
    )(pstart, counts, padded, e_flat, r_flat, h3, zrow)


def _expert_kernel(bexp_ref, nact_ref, x_ref, w1_ref, w3_ref, w2_ref, o_ref):
    i = pl.program_id(0)

    @pl.when(i < nact_ref[0])
    def _():
        x = x_ref[...]
        a = jnp.dot(x, w1_ref[0], preferred_element_type=F32)
        b = jnp.dot(x, w3_ref[0], preferred_element_type=F32)
        hid = (_silu(a) * b).astype(BF16)
        o_ref[...] = jnp.dot(hid, w2_ref[0], preferred_element_type=F32).astype(BF16)


def _experts(xs, bexp, nact, w1, w3, w2):
    p, d = xs.shape
    de = w1.shape[2]
    nb = p // MOE_ROWS
    blk = lambda i, be, na: (jnp.minimum(i, na[0] - 1), 0)
    wmap = lambda i, be, na: (be[jnp.minimum(i, na[0] - 1)], 0, 0)
    grid_spec = pltpu.PrefetchScalarGridSpec(
        num_scalar_prefetch=2,
        grid=(nb,),
        in_specs=[pl.BlockSpec((MOE_ROWS, d), blk),
                  pl.BlockSpec((1, d, de), wmap), pl.BlockSpec((1, d, de), wmap),
                  pl.BlockSpec((1, de, d), wmap)],
        out_specs=pl.BlockSpec((MOE_ROWS, d), blk))
    return pl.pallas_call(
        _expert_kernel,
        out_shape=jax.ShapeDtypeStruct((p, d), BF16),
        grid_spec=grid_spec,
        compiler_params=_cparams(("arbitrary",)),
        name="moe_experts",
    )(bexp, nact, xs, w1, w3, w2)


def _combine_kernel(pstart_ref, e_ref, r_ref, gt_ref, ys_ref, x1_ref, h2_ref, ws1_ref, ws3_ref, ws2_ref, g2_ref,
                    o_ref, buf_ref, routed_ref, sem, *, tm):
    def issue(t, c):
        for k in range(TOPK):
            a = t * TOPK + k
            src = pstart_ref[e_ref[a]] + r_ref[a]
            pltpu.make_async_copy(ys_ref.at[src], buf_ref.at[a], sem).start()
        return c
    lax.fori_loop(0, tm, issue, 0)

    h2 = h2_ref[...]
    a1 = jnp.dot(h2, ws1_ref[...], preferred_element_type=F32)
    a3 = jnp.dot(h2, ws3_ref[...], preferred_element_type=F32)
    shared = jnp.dot((_silu(a1) * a3).astype(BF16), ws2_ref[...], preferred_element_type=F32)

    def wait_tok(t, c):
        for k in range(TOPK):
            pltpu.make_async_copy(ys_ref.at[0], buf_ref.at[0], sem).wait()
        return c
    lax.fori_loop(0, tm, wait_tok, 0)

    nsub = buf_ref.shape[1]

    def reduce_tok(t, c):
        acc = jnp.zeros((nsub, LANES), F32)
        for k in range(TOPK):
            a = t * TOPK + k
            acc = acc + gt_ref[a] * buf_ref[a].astype(F32)
        routed_ref[pl.ds(pl.multiple_of(t * nsub, nsub), nsub), :] = acc
        return c
    lax.fori_loop(0, tm, reduce_tok, 0)

    routed = jnp.concatenate([routed_ref[pl.ds(j, tm, stride=nsub), :] for j in range(nsub)], axis=1)
    o_ref[...] = x1_ref[...] + g2_ref[0] * (routed + shared)


def _combine(ys3, e_flat, r_flat, g_flat, pstart, x1, h2, ws1, ws3, ws2, g2, seq, tm=256):
    n, d = x1.shape
    tpb = seq // tm
    nsub = ys3.shape[1]
    smem_blk = pl.BlockSpec((tm * TOPK,), lambda i, *_: (i,), memory_space=pltpu.SMEM)
    row = lambda i, *_: (i, 0)
    const = lambda i, *_: (0, 0)
    grid_spec = pltpu.PrefetchScalarGridSpec(
        num_scalar_prefetch=1,
        grid=(n // tm,),
        in_specs=[smem_blk, smem_blk, smem_blk,
                  pl.BlockSpec(memory_space=pl.ANY),
                  pl.BlockSpec((tm, d), row), pl.BlockSpec((tm, d), row),
                  pl.BlockSpec(ws1.shape, const), pl.BlockSpec(ws3.shape, const), pl.BlockSpec(ws2.shape, const),
                  pl.BlockSpec((1, 1, d), lambda i, *_: (i // tpb, 0, 0))],
        out_specs=pl.BlockSpec((tm, d), row),
        scratch_shapes=[pltpu.VMEM((tm * TOPK, nsub, LANES), ys3.dtype),
                        pltpu.VMEM((tm * nsub, LANES), F32),
                        pltpu.SemaphoreType.DMA(())])
    return pl.pallas_call(
        functools.partial(_combine_kernel, tm=tm),
        out_shape=jax.ShapeDtypeStruct((n, d), F32),
        grid_spec=grid_spec,
        compiler_params=_cparams(("arbitrary",)),
        name="moe_combine",
    )(pstart, e_flat, r_flat, g_flat, ys3, x1, h2, ws1, ws3, ws2, g2)


def _moe_plan(counts_f, n_assign):
    counts = counts_f[0, :N_EXPERTS].astype(I32)
    padded = (counts + MOE_ROWS - 1) // MOE_ROWS * MOE_ROWS
    pend = jnp.cumsum(padded)
    pstart = pend - padded
    n_blocks = (n_assign + N_EXPERTS * (MOE_ROWS - 1) + MOE_ROWS - 1) // MOE_ROWS
    bexp = jnp.minimum(jnp.searchsorted(pend, jnp.arange(n_blocks, dtype=I32) * MOE_ROWS, side='right'),
                       N_EXPERTS - 1).astype(I32)
    nact = (pend[-1:] // MOE_ROWS).astype(I32)
    return counts, padded, pstart.astype(I32), bexp, nact, n_blocks


def _moe_layer(x1, h2, eidx, gate, rank, counts_f, w1, w3, w2, ws1, ws3, ws2, g2, seq):
    n, d = x1.shape
    nsub = d // LANES
    counts, padded, pstart, bexp, nact, n_blocks = _moe_plan(counts_f, n * TOPK)
    e_flat = eidx[:, :TOPK].reshape(-1)
    r_flat = rank[:, :TOPK].reshape(-1)
    g_flat = gate[:, :TOPK].reshape(-1)
    xs3 = _dispatch(h2.reshape(n, nsub, LANES), e_flat, r_flat, pstart, counts, padded, n_blocks * MOE_ROWS)
    ys = _experts(xs3.reshape(n_blocks * MOE_ROWS, d), bexp, nact, w1, w3, w2)
    return _combine(ys.reshape(n_blocks * MOE_ROWS, nsub, LANES), e_flat, r_flat, g_flat, pstart,
                    x1, h2, ws1, ws3, ws2, g2, seq)


def kernel(x, c, positions, ada_w, ada_b, norm_mix_g, norm_ffn_g, rel_bias, even_w_in, ret_norm_g, dsa_q_g, dsa_k_g, even_w_out, ssm_w_in, ssm_conv_w, ssm_conv_b, ssm_dt_bias, ssm_a_log, ssm_d, ssm_norm_g, ssm_w_out, router_w, router_b, exp_w1, exp_w3, exp_w2, sh_w1, sh_w3, sh_w2):
    bsz, seq, d = x.shape
    n = bsz * seq
    depth = ada_w.shape[0]
    n_e = router_w.shape[2]
    d_inner = ssm_w_out.shape[1]
    xc = x.reshape(n, d)
    for i in range(depth):
        j = i // 2
        mod = _adaln_mod(c, ada_w[i], ada_b[i])
        sh1, sc1, g1, sh2, sc2, g2 = [m.reshape(bsz, 1, d) for m in jnp.split(mod, 6, axis=-1)]
        if i % 2 == 0:
            w_in = even_w_in[j]
            w_pad = jnp.pad(w_in, ((0, 0), (0, 3072 - w_in.shape[1]))).astype(BF16)
            ret_in, qa, kn, vaug, qi, kw = _inproj_even(xc, sc1, sh1, norm_mix_g[i], w_pad, dsa_k_g[j], seq)
            y_ret = _retention(ret_in.reshape(bsz, seq, 2048), positions.reshape(bsz, seq, 1), ret_norm_g[j])
            y_att = _dsa(qa.reshape(bsz, seq, 512), qi.reshape(bsz, seq, 256), kw.reshape(bsz, seq, LANES),
                         kn.reshape(bsz, seq, DSA_DH), vaug.reshape(bsz, seq, LANES), positions, rel_bias,
                         dsa_q_g[j])
            wo = even_w_out[j].astype(BF16)
            ys = [y_ret.reshape(n, 512), y_att.reshape(n, 512)]
            ws = [wo[:512], wo[512:]]
        else:
            w_in = ssm_w_in[j]
            n_main = w_in.shape[1] - d_inner // SSM_HEADDIM
            w_main = w_in[:, :n_main].astype(BF16)
            w_dt = jnp.pad(w_in[:, n_main:], ((0, 0), (0, LANES - (w_in.shape[1] - n_main)))).astype(BF16)
            zx, dt_raw = _inproj_ssm(xc, sc1, sh1, norm_mix_g[i], w_main, w_dt, seq)
            y = _ssd(zx.reshape(bsz, seq, n_main), dt_raw.reshape(bsz, seq, LANES), ssm_conv_w[j], ssm_conv_b[j],
                     ssm_dt_bias[j], ssm_a_log[j], ssm_d[j], ssm_norm_g[j], d_inner)
            ys = [y.reshape(n, d_inner)]
            ws = [ssm_w_out[j].astype(BF16)]
        rw_pad = jnp.pad(router_w[i], ((0, 0), (0, LANES - n_e))).astype(BF16)
        rb_pad = jnp.pad(router_b[i], (0, LANES - n_e)).reshape(1, LANES)
        x1, h2, eidx, gate, rank, cnt = _outproj_route(xc, ys, ws, g1, sc2, sh2, norm_ffn_g[i], rw_pad, rb_pad, seq)
        xc = _moe_layer(x1, h2, eidx, gate, rank, cnt, exp_w1[i].astype(BF16), exp_w3[i].astype(BF16),
                        exp_w2[i].astype(BF16), sh_w1[i].astype(BF16), sh_w3[i].astype(BF16),
                        sh_w2[i].astype(BF16), g2, seq)
    return xc.reshape(bsz, seq, d)
```

```python
import functools
import math

import jax
import jax.numpy as jnp
from jax import lax
from jax.experimental import pallas as pl
from jax.experimental.pallas import tpu as pltpu

F32 = jnp.float32
BF16 = jnp.bfloat16
I32 = jnp.int32

EPS = 1e-6
RET_HEADS = 4
RET_DK = 128
RET_DV = 128
RET_CHUNK = 128
ROPE_THETA = 10000.0
DSA_HEADS = 8
DSA_DH = 64
IDX_HEADS = 8
IDX_DH = 32
DSA_TOPK_MAX = 256
REL_BUCKETS = 32
REL_MAX_DIST = 128
SSM_HEADDIM = 64
SSM_GROUPS = 8
SSM_STATE = 128
SSM_CONV = 4
SSM_CHUNK = 128
N_EXPERTS = 64
TOPK = 8
ROUTED_SCALE = 2.5

LANES = 128
VMEM_LIMIT = 48 * 1024 * 1024
NEG_BIG = -1e30
INT_MIN = -(2 ** 31)
M_NEGINF = -2139095041


def _cparams(sem):
    return pltpu.CompilerParams(dimension_semantics=sem, vmem_limit_bytes=VMEM_LIMIT)


def _silu(v):
    return v * (1.0 / (1.0 + jnp.exp(-v)))


def _sigmoid(v):
    return 1.0 / (1.0 + jnp.exp(-v))


def _mod_kernel(c_ref, w_ref, b_ref, o_ref):
    cond = _silu(c_ref[...]).astype(BF16)
    o_ref[...] = jnp.dot(cond, w_ref[...].astype(BF16), preferred_element_type=F32) + b_ref[...]


def _adaln_mod(c, w, b):
    bsz, d = c.shape
    n_out = w.shape[1]
    tn = 1024
    return pl.pallas_call(
        _mod_kernel,
        out_shape=jax.ShapeDtypeStruct((bsz, n_out), F32),
        grid=(n_out // tn,),
        in_specs=[pl.BlockSpec((bsz, d), lambda j: (0, 0)),
                  pl.BlockSpec((d, tn), lambda j: (0, j)),
                  pl.BlockSpec((1, tn), lambda j: (0, j))],
        out_specs=pl.BlockSpec((bsz, tn), lambda j: (0, j)),
        compiler_params=_cparams(("arbitrary",)),
        name="adaln_mod",
    )(c, w, b.reshape(1, n_out))


def _norm_mod(x, g, sc, sh):
    ms = jnp.mean(x * x, axis=-1, keepdims=True)
    return (x * lax.rsqrt(ms + EPS) * g) * (1.0 + sc) + sh


def _inproj_even_kernel(x_ref, sc_ref, sh_ref, g_ref, w_ref, kg_ref,
                        ret_ref, qa_ref, kn_ref, va_ref, qi_ref, kw_ref):
    h = _norm_mod(x_ref[...], g_ref[...], sc_ref[0], sh_ref[0])
    p = jnp.dot(h.astype(BF16), w_ref[...], preferred_element_type=F32)
    ret_ref[...] = p[:, :2048].astype(BF16)
    qa_ref[...] = p[:, 2048:2560].astype(BF16)
    kv = p[:, 2560:2688]
    ka = kv[:, :DSA_DH]
    kn = ka * lax.rsqrt(jnp.mean(ka * ka, axis=-1, keepdims=True) + EPS) * kg_ref[...]
    kn_ref[...] = kn.astype(BF16)
    lane = lax.broadcasted_iota(I32, kv.shape, 1)
    rolled = pltpu.roll(kv, DSA_DH, axis=1)
    vaug = jnp.where(lane < DSA_DH, rolled, jnp.where(lane == DSA_DH, 1.0, 0.0))
    va_ref[...] = vaug.astype(BF16)
    qi_ref[...] = p[:, 2688:2944].astype(BF16)
    kw_ref[...] = p[:, 2944:3072]


def _inproj_even(x2, sc, sh, g, w_pad, k_gain, seq, tm=512):
    n, d = x2.shape
    tpb = seq // tm
    cols = w_pad.shape[1]
    bmap = lambda i: (i // tpb, 0, 0)
    row = lambda i: (i, 0)
    const = lambda i: (0, 0)
    outs = (jax.ShapeDtypeStruct((n, 2048), BF16), jax.ShapeDtypeStruct((n, 512), BF16),
            jax.ShapeDtypeStruct((n, DSA_DH), BF16), jax.ShapeDtypeStruct((n, LANES), BF16),
            jax.ShapeDtypeStruct((n, 256), BF16), jax.ShapeDtypeStruct((n, LANES), F32))
    return pl.pallas_call(
        _inproj_even_kernel,
        out_shape=outs,
        grid=(n // tm,),
        in_specs=[pl.BlockSpec((tm, d), row),
                  pl.BlockSpec((1, 1, d), bmap), pl.BlockSpec((1, 1, d), bmap),
                  pl.BlockSpec((1, d), const),
                  pl.BlockSpec((d, cols), const),
                  pl.BlockSpec((1, DSA_DH), const)],
        out_specs=(pl.BlockSpec((tm, 2048), row), pl.BlockSpec((tm, 512), row),
                   pl.BlockSpec((tm, DSA_DH), row), pl.BlockSpec((tm, LANES), row),
                   pl.BlockSpec((tm, 256), row), pl.BlockSpec((tm, LANES), row)),
        compiler_params=_cparams(("arbitrary",)),
        name="inproj_even",
    )(x2, sc, sh, g.reshape(1, d), w_pad, k_gain.reshape(1, DSA_DH))


def _retention_kernel(in_ref, pos_ref, inv_ref, sgn_ref, dintra_ref, xi_ref, zeta_ref, cd_ref, g_ref,
                      o_ref, r_ref):
    c = pl.program_id(1)

    @pl.when(c == 0)
    def _():
        r_ref[...] = jnp.zeros_like(r_ref)

    blk = in_ref[0]
    posf = pos_ref[0].astype(F32)
    ang = posf * inv_ref[...]
    cos2 = jnp.cos(ang)
    sin2 = jnp.sin(ang) * sgn_ref[...]
    half = RET_DK // 2
    outs = []
    for h in range(RET_HEADS):
        q = blk[:, h * RET_DK:(h + 1) * RET_DK].astype(F32)
        k = blk[:, 512 + h * RET_DK:512 + (h + 1) * RET_DK].astype(F32)
        v = blk[:, 1024 + h * RET_DV:1024 + (h + 1) * RET_DV]
        gate = blk[:, 1536 + h * RET_DV:1536 + (h + 1) * RET_DV].astype(F32)
        q = q * cos2 + pltpu.roll(q, half, axis=1) * sin2
        k = (k * cos2 + pltpu.roll(k, half, axis=1) * sin2) * (RET_DK ** -0.5)
        qb = q.astype(BF16)
        kb = k.astype(BF16)
        inner = lax.dot_general(qb, kb, (((1,), (1,)), ((), ())), preferred_element_type=F32)
        inner = inner * dintra_ref[h]
        y = jnp.dot(inner.astype(BF16), v, preferred_element_type=F32)
        r_prev = r_ref[h]
        y = y + jnp.dot((q * xi_ref[h]).astype(BF16), r_prev.astype(BF16), preferred_element_type=F32)
        kz = (k * zeta_ref[h]).astype(BF16)
        u = lax.dot_general(kz, v, (((0,), (0,)), ((), ())), preferred_element_type=F32)
        r_ref[h] = r_prev * cd_ref[h] + u
        yn = y * lax.rsqrt(jnp.mean(y * y, axis=-1, keepdims=True) + EPS) * g_ref[h]
        outs.append((yn * _silu(gate)).astype(BF16))
    o_ref[0] = jnp.concatenate(outs, axis=1)


def _retention(ret_in, pos3, ret_norm_g):
    bsz, seq, _ = ret_in.shape
    L = RET_CHUNK
    nc = seq // L
    H = RET_HEADS
    half = RET_DK // 2
    inv = ROPE_THETA ** (-jnp.arange(half, dtype=F32) / half)
    inv2 = jnp.concatenate([inv, inv]).reshape(1, RET_DK)
    sgn = jnp.concatenate([-jnp.ones((half,), F32), jnp.ones((half,), F32)]).reshape(1, RET_DK)
    log_g = jnp.log(1.0 - 2.0 ** (-5.0 - jnp.arange(H, dtype=F32)))
    idx = jnp.arange(L, dtype=F32)
    rel = idx[:, None] - idx[None, :]
    d_intra = jnp.where(rel >= 0, jnp.exp(log_g[:, None, None] * jnp.maximum(rel, 0.0)), 0.0)
    zeta = jnp.exp(log_g[:, None] * (L - 1 - idx))
    xi = jnp.exp(log_g[:, None] * (idx + 1))
    cd = jnp.exp(log_g * L)
    xi_b = jnp.broadcast_to(xi[:, :, None], (H, L, RET_DK))
    zeta_b = jnp.broadcast_to(zeta[:, :, None], (H, L, RET_DK))
    cd_b = jnp.broadcast_to(cd[:, None, None], (H, 1, RET_DV))
    c3 = lambda b, c: (0, 0, 0)
    c2 = lambda b, c: (0, 0)
    return pl.pallas_call(
        _retention_kernel,
        out_shape=jax.ShapeDtypeStruct((bsz, seq, H * RET_DV), BF16),
        grid=(bsz, nc),
        in_specs=[pl.BlockSpec((1, L, 2048), lambda b, c: (b, c, 0)),
                  pl.BlockSpec((1, L, 1), lambda b, c: (b, c, 0)),
                  pl.BlockSpec((1, RET_DK), c2), pl.BlockSpec((1, RET_DK), c2),
                  pl.BlockSpec((H, L, L), c3), pl.BlockSpec((H, L, RET_DK), c3),
                  pl.BlockSpec((H, L, RET_DK), c3), pl.BlockSpec((H, 1, RET_DV), c3),
                  pl.BlockSpec((H, 1, RET_DV), c3)],
        out_specs=pl.BlockSpec((1, L, H * RET_DV), lambda b, c: (b, c, 0)),
        scratch_shapes=[pltpu.VMEM((H, RET_DK, RET_DV), F32)],
        compiler_params=_cparams(("arbitrary", "arbitrary")),
        name="retention",
    )(ret_in, pos3, inv2, sgn, d_intra, xi_b, zeta_b, cd_b, ret_norm_g.reshape(H, 1, RET_DV))


def _dsa_kernel(minq_ref, maxk_ref,
                qa_ref, qi_ref, kwq_ref, posq_ref, kit_ref, kt_ref, va_ref, posk_ref,
                qg_ref, btab_ref, bfar_ref,
                o_ref,
                qs_ref, qis_ref, wb_ref, m_ref, lg_ref, p_ref, al_ref, mrun_ref, acc_ref,
                *, qb_size, topk, idx_bits):
    QB = qb_size
    H = DSA_HEADS
    b = pl.program_id(0)
    qb = pl.program_id(1)
    nq = pl.num_programs(1)
    nk = qb + 1
    q0 = qb * QB

    qa = qa_ref[0].astype(F32)
    qi = qi_ref[0]
    kwq = kwq_ref[0]
    for h in range(H):
        qh = qa[:, h * DSA_DH:(h + 1) * DSA_DH]
        qh = qh * lax.rsqrt(jnp.mean(qh * qh, axis=-1, keepdims=True) + EPS) * qg_ref[...]
        qs_ref[h * QB:(h + 1) * QB, :] = (qh * (DSA_DH ** -0.5)).astype(BF16)
        qis_ref[h * QB:(h + 1) * QB, :] = qi[:, h * IDX_DH:(h + 1) * IDX_DH]
        w = kwq[:, IDX_DH + h:IDX_DH + h + 1] * (IDX_HEADS ** -0.5 * IDX_DH ** -0.5)
        wb_ref[h] = jnp.broadcast_to(w, (QB, LANES))

    nrep = QB // LANES

    def rep(t):
        return jnp.concatenate([t] * nrep, axis=1)

    qidx = q0 + lax.broadcasted_iota(I32, (QB, QB), 0)
    lane_k = lax.broadcasted_iota(I32, (QB, QB), 1)

    def score_chunk(kc, carry):
        k0 = pl.multiple_of(kc * QB, QB)
        a = jnp.dot(qis_ref[...], kit_ref[0, :, pl.ds(k0, QB)], preferred_element_type=F32)
        s = jnp.zeros((QB, QB), F32)
        for h in range(H):
            s = s + jnp.maximum(a[h * QB:(h + 1) * QB], 0.0) * rep(wb_ref[h])
        s = jnp.where(k0 + lane_k <= qidx, s, -jnp.inf)
        bits = pltpu.bitcast(s, I32)
        m_ref[:, pl.ds(k0, QB)] = jnp.where(bits < 0, bits ^ 0x7FFFFFFF, bits)
        return carry

    lax.fori_loop(0, nk, score_chunk, 0)

    ones_k = jnp.ones((QB, LANES), BF16)

    def count(pred):
        def body(kc, cnt):
            k0 = pl.multiple_of(kc * QB, QB)
            msk = pred(m_ref[:, pl.ds(k0, QB)], k0 + lane_k)
            return cnt + jnp.dot(jnp.where(msk, 1.0, 0.0).astype(BF16), ones_k, preferred_element_type=F32)
        return lax.fori_loop(0, nk, body, jnp.zeros((QB, LANES), F32))

    kf = float(topk)

    def bit_body(i, ans):
        cand = ans + jnp.left_shift(jnp.int32(1), 31 - i)
        cand_r = rep(cand)
        cnt = count(lambda m, kidx: m >= cand_r)
        return jnp.where(cnt >= kf, cand, ans)

    ans = lax.fori_loop(0, 32, bit_body, jnp.full((QB, LANES), INT_MIN, I32))
    ans_r = rep(ans)

    ans1_r = rep(ans + 1)
    c_gt = count(lambda m, kidx: m >= ans1_r)
    c_ge = count(lambda m, kidx: m >= ans_r)
    need = kf - c_gt
    tie = jnp.logical_and(c_ge > kf, ans > M_NEGINF)
    any_tie = jnp.max(jnp.where(tie, 1, 0))

    def tie_search():
        def jbody(i, jc):
            cand = jc + jnp.left_shift(jnp.int32(1), idx_bits - 1 - i)
            cand_r = rep(cand)
            c = count(lambda m, kidx: jnp.logical_and(m == ans_r, kidx < cand_r))
            return jnp.where(c <= need, cand, jc)
        return lax.fori_loop(0, idx_bits, jbody, jnp.zeros((QB, LANES), I32))

    jcut = lax.cond(any_tie > 0, tie_search, lambda: jnp.full((QB, LANES), 2 ** 30, I32))
    jcut_r = rep(jcut)

    mrun_ref[...] = jnp.full(mrun_ref.shape, NEG_BIG, F32)
    acc_ref[...] = jnp.zeros_like(acc_ref)
    posq = posq_ref[0]
    min_q = minq_ref[b * nq + qb]

    def att_chunk(kc, carry):
        k0 = pl.multiple_of(kc * QB, QB)
        m = m_ref[:, pl.ds(k0, QB)]
        kidx = k0 + lane_k
        sel = jnp.logical_or(m > ans_r, jnp.logical_and(m == ans_r, kidx < jcut_r))
        sel = jnp.logical_and(sel, kidx <= qidx)
        lg = jnp.dot(qs_ref[...], kt_ref[0, :, pl.ds(k0, QB)], preferred_element_type=F32)
        far = (min_q - maxk_ref[b * nq + kc]) >= REL_MAX_DIST

        @pl.when(far)
        def _():
            for h in range(H):
                bias = rep(jnp.broadcast_to(bfar_ref[h:h + 1, :], (QB, LANES)))
                lg_ref[h * QB:(h + 1) * QB, :] = jnp.where(sel, lg[h * QB:(h + 1) * QB] + bias, NEG_BIG)

        @pl.when(jnp.logical_not(far))
        def _():
            n = jnp.maximum(posq - posk_ref[0, :, pl.ds(k0, QB)], 0)
            is_far = n >= REL_MAX_DIST
            ncl = jnp.minimum(n, REL_MAX_DIST - 1)
            for h in range(H):
                tbl = jnp.broadcast_to(btab_ref[h:h + 1, :], (QB, LANES))
                near = jnp.concatenate(
                    [jnp.take_along_axis(tbl, ncl[:, j * LANES:(j + 1) * LANES], axis=1) for j in range(nrep)],
                    axis=1)
                bias = jnp.where(is_far, rep(jnp.broadcast_to(bfar_ref[h:h + 1, :], (QB, LANES))), near)
                lg_ref[h * QB:(h + 1) * QB, :] = jnp.where(sel, lg[h * QB:(h + 1) * QB] + bias, NEG_BIG)

        for h in range(H):
            rows = slice(h * QB, (h + 1) * QB)
            lgh = lg_ref[rows, :]
            m_old = mrun_ref[rows, :]
            m_new = jnp.maximum(m_old, jnp.max(lgh, axis=1, keepdims=True))
            al_ref[rows, :] = jnp.exp(m_old - m_new)
            mrun_ref[rows, :] = m_new
            p_ref[rows, :] = jnp.exp(lgh - rep(m_new)).astype(BF16)
        pv = jnp.dot(p_ref[...], va_ref[0, pl.ds(k0, QB), :], preferred_element_type=F32)
        acc_ref[...] = al_ref[...] * acc_ref[...] + pv
        return carry

    lax.fori_loop(0, nk, att_chunk, 0)

    outs = []
    for h in range(H):
        a = acc_ref[h * QB:(h + 1) * QB, :]
        outs.append(a[:, :DSA_DH] / a[:, DSA_DH:DSA_DH + 1])
    o_ref[0] = jnp.concatenate(outs, axis=1).astype(BF16)


def _t5_bucket(n):
    max_exact = REL_BUCKETS // 2
    nf = jnp.maximum(n, 1).astype(F32)
    large = max_exact + (jnp.log(nf / max_exact) / math.log(REL_MAX_DIST / max_exact)
                         * (REL_BUCKETS - max_exact)).astype(I32)
    large = jnp.minimum(large, REL_BUCKETS - 1)
    return jnp.where(n < max_exact, n, large)


def _dsa(qa, qi, kw, kn, vaug, positions, rel_bias, q_gain, qb_size=256):
    bsz, seq, _ = qa.shape
    QB = min(qb_size, seq)
    nq = seq // QB
    H = DSA_HEADS
    topk = min(DSA_TOPK_MAX, seq // 4)
    idx_bits = int(math.log2(seq)) + 1
    kt = jnp.swapaxes(kn, 1, 2)
    kit = jnp.swapaxes(kw[:, :, :IDX_DH].astype(BF16), 1, 2)
    posq = positions.reshape(bsz, seq, 1)
    posk = positions.reshape(bsz, 1, seq)
    pblk = positions.reshape(bsz, nq, QB)
    minq = jnp.min(pblk, axis=-1).reshape(-1)
    maxk = jnp.max(pblk, axis=-1).reshape(-1)
    btab = rel_bias[_t5_bucket(jnp.arange(REL_MAX_DIST, dtype=I32))].T
    bfar = jnp.broadcast_to(rel_bias[REL_BUCKETS - 1][:, None], (H, LANES))
    blk_q = lambda b, q, *_: (b, q, 0)
    per_b = lambda b, q, *_: (b, 0, 0)
    const = lambda b, q, *_: (0, 0)
    kern = functools.partial(_dsa_kernel, qb_size=QB, topk=topk, idx_bits=idx_bits)
    grid_spec = pltpu.PrefetchScalarGridSpec(
        num_scalar_prefetch=2,
        grid=(bsz, nq),
        in_specs=[pl.BlockSpec((1, QB, 512), blk_q), pl.BlockSpec((1, QB, 256), blk_q),
                  pl.BlockSpec((1, QB, LANES), blk_q), pl.BlockSpec((1, QB, 1), blk_q),
                  pl.BlockSpec((1, IDX_DH, seq), per_b), pl.BlockSpec((1, DSA_DH, seq), per_b),
                  pl.BlockSpec((1, seq, LANES), per_b), pl.BlockSpec((1, 1, seq), per_b),
                  pl.BlockSpec((1, DSA_DH), const), pl.BlockSpec((H, LANES), const),
                  pl.BlockSpec((H, LANES), const)],
        out_specs=pl.BlockSpec((1, QB, H * DSA_DH), blk_q),
        scratch_shapes=[pltpu.VMEM((H * QB, DSA_DH), BF16), pltpu.VMEM((H * QB, IDX_DH), BF16),
                        pltpu.VMEM((H, QB, LANES), F32), pltpu.VMEM((QB, seq), I32),
                        pltpu.VMEM((H * QB, QB), F32), pltpu.VMEM((H * QB, QB), BF16),
                        pltpu.VMEM((H * QB, LANES), F32), pltpu.VMEM((H * QB, LANES), F32),
                        pltpu.VMEM((H * QB, LANES), F32)])
    return pl.pallas_call(
        kern,
        out_shape=jax.ShapeDtypeStruct((bsz, seq, H * DSA_DH), BF16),
        grid_spec=grid_spec,
        compiler_params=_cparams(("arbitrary", "arbitrary")),
        name="dsa",
    )(minq, maxk, qa, qi, kw, posq, kit, kt, vaug, posk, q_gain.reshape(1, DSA_DH), btab, bfar)


def _inproj_ssm_kernel(x_ref, sc_ref, sh_ref, g_ref, w_ref, wdt_ref, zx_ref, dt_ref, h_ref):
    j = pl.program_id(1)

    @pl.when(j == 0)
    def _():
        h = _norm_mod(x_ref[...], g_ref[...], sc_ref[0], sh_ref[0]).astype(BF16)
        h_ref[...] = h
        dt_ref[...] = jnp.dot(h, wdt_ref[...], preferred_element_type=F32)

    zx_ref[...] = jnp.dot(h_ref[...], w_ref[...], preferred_element_type=F32).astype(BF16)


def _inproj_ssm(x2, sc, sh, g, w_main, w_dt, seq, tm=512, tn=2048):
    n, d = x2.shape
    tpb = seq // tm
    cols = w_main.shape[1]
    bmap = lambda i, j: (i // tpb, 0, 0)
    return pl.pallas_call(
        _inproj_ssm_kernel,
        out_shape=(jax.ShapeDtypeStruct((n, cols), BF16), jax.ShapeDtypeStruct((n, LANES), F32)),
        grid=(n // tm, cols // tn),
        in_specs=[pl.BlockSpec((tm, d), lambda i, j: (i, 0)),
                  pl.BlockSpec((1, 1, d), bmap), pl.BlockSpec((1, 1, d), bmap),
                  pl.BlockSpec((1, d), lambda i, j: (0, 0)),
                  pl.BlockSpec((d, tn), lambda i, j: (0, j)),
                  pl.BlockSpec((d, LANES), lambda i, j: (0, 0))],
        out_specs=(pl.BlockSpec((tm, tn), lambda i, j: (i, j)), pl.BlockSpec((tm, LANES), lambda i, j: (i, 0))),
        scratch_shapes=[pltpu.VMEM((tm, d), BF16)],
        compiler_params=_cparams(("arbitrary", "arbitrary")),
        name="inproj_ssm",
    )(x2, sc, sh, g.reshape(1, d), w_main, w_dt)


def _split3(a):
    hi = a.astype(BF16)
    r1 = a - hi.astype(F32)
    mid = r1.astype(BF16)
    lo = (r1 - mid.astype(F32)).astype(BF16)
    return hi, mid, lo


def _ssd_kernel(zx_ref, dt_ref, dtt_ref, cw_ref, cb_ref, dtb_ref, dtbt_ref, a_ref, at_ref, dsk_ref, ng_ref,
                o_ref, state_ref, tail_ref, *, d_inner, n_heads):
    L = SSM_CHUNK
    P = SSM_HEADDIM
    NS = SSM_STATE
    G = SSM_GROUPS
    R = n_heads // G
    gn = G * NS
    c = pl.program_id(1)

    @pl.when(c == 0)
    def _():
        state_ref[...] = jnp.zeros_like(state_ref)
        tail_ref[...] = jnp.zeros_like(tail_ref)

    blk = zx_ref[0]
    z = blk[:, :d_inner].astype(F32)
    xin = blk[:, d_inner:].astype(F32)
    tail = tail_ref[...]
    row8 = lax.broadcasted_iota(I32, (8, xin.shape[1]), 0)
    conv = cb_ref[...] + cw_ref[SSM_CONV - 1:SSM_CONV, :] * xin
    for s in range(1, SSM_CONV):
        rolled = pltpu.roll(xin, s, axis=0)
        top = jnp.where(row8 < s, pltpu.roll(tail, s, axis=0), rolled[:8])
        shifted = jnp.concatenate([top, rolled[8:]], axis=0)
        conv = conv + cw_ref[SSM_CONV - 1 - s:SSM_CONV - s, :] * shifted
    tail_ref[...] = xin[L - 8:, :]
    xbc = _silu(conv)
    xs = xbc[:, :d_inner]
    bm = xbc[:, d_inner:d_inner + gn]
    cm = xbc[:, d_inner + gn:]

    def softplus(v):
        return jnp.maximum(v, 0.0) + jnp.log(1.0 + jnp.exp(-jnp.abs(v)))

    dt = softplus(dt_ref[0][:, :n_heads] + dtb_ref[...])
    dtt = softplus(dtt_ref[0] + dtbt_ref[...])
    da = dt * (-jnp.exp(a_ref[...]))
    dat = dtt * (-jnp.exp(at_ref[...]))
    r_i = lax.broadcasted_iota(I32, (L, L), 0)
    c_i = lax.broadcasted_iota(I32, (L, L), 1)
    causal = c_i <= r_i
    tril = jnp.where(causal, 1.0, 0.0).astype(BF16)
    triu = jnp.where(r_i <= c_i, 1.0, 0.0).astype(BF16)
    cs = sum(jnp.dot(tril, t, preferred_element_type=F32) for t in _split3(da))
    cst = sum(jnp.dot(t, triu, preferred_element_type=F32) for t in _split3(dat))
    cs_last = cs[L - 1:L, :]
    e_cs = jnp.exp(cs)
    e_end = jnp.exp(cs_last - cs)
    e_chunk = jnp.exp(cs_last)

    ys = []
    for g in range(G):
        bg = bm[:, g * NS:(g + 1) * NS].astype(BF16)
        cg = cm[:, g * NS:(g + 1) * NS].astype(BF16)
        cbm = lax.dot_general(cg, bg, (((1,), (1,)), ((), ())), preferred_element_type=F32)
        for r in range(R):
            h = g * R + r
            xh = xs[:, h * P:(h + 1) * P]
            xdt = xh * dt[:, h:h + 1]
            seg = cs[:, h:h + 1] - cst[h:h + 1, :]
            decay = jnp.exp(jnp.where(causal, seg, -jnp.inf))
            y = jnp.dot((cbm * decay).astype(BF16), xdt.astype(BF16), preferred_element_type=F32)
            st = state_ref[h]
            y_off = lax.dot_general(cg, st.astype(BF16), (((1,), (1,)), ((), ())), preferred_element_type=F32)
            y = y + y_off * e_cs[:, h:h + 1]
            upd = lax.dot_general((xdt * e_end[:, h:h + 1]).astype(BF16), bg, (((0,), (0,)), ((), ())),
                                  preferred_element_type=F32)
            state_ref[h] = st * e_chunk[:, h:h + 1] + upd
            ys.append(y + xh * dsk_ref[:, h:h + 1])
    y = jnp.concatenate(ys, axis=1) * _silu(z)
    gw = d_inner // G
    outs = []
    for g in range(G):
        yg = y[:, g * gw:(g + 1) * gw]
        outs.append(yg * lax.rsqrt(jnp.mean(yg * yg, axis=-1, keepdims=True) + EPS))
    o_ref[0] = (jnp.concatenate(outs, axis=1) * ng_ref[...]).astype(BF16)


def _ssd(zx, dt_raw, conv_w, conv_b, dt_bias, a_log, d_skip, norm_g, d_inner):
    bsz, seq, width = zx.shape
    L = SSM_CHUNK
    nc = seq // L
    H = d_inner // SSM_HEADDIM
    conv_dim = width - d_inner
    dtt = jnp.swapaxes(dt_raw[:, :, :H], 1, 2)
    c2 = lambda b, c: (0, 0)
    kern = functools.partial(_ssd_kernel, d_inner=d_inner, n_heads=H)
    return pl.pallas_call(
        kern,
        out_shape=jax.ShapeDtypeStruct((bsz, seq, d_inner), BF16),
        grid=(bsz, nc),
        in_specs=[pl.BlockSpec((1, L, width), lambda b, c: (b, c, 0)),
                  pl.BlockSpec((1, L, LANES), lambda b, c: (b, c, 0)),
                  pl.BlockSpec((1, H, L), lambda b, c: (b, 0, c)),
                  pl.BlockSpec((SSM_CONV, conv_dim), c2), pl.BlockSpec((1, conv_dim), c2),
                  pl.BlockSpec((1, H), c2), pl.BlockSpec((H, 1), c2),
                  pl.BlockSpec((1, H), c2), pl.BlockSpec((H, 1), c2),
                  pl.BlockSpec((1, H), c2), pl.BlockSpec((1, d_inner), c2)],
        out_specs=pl.BlockSpec((1, L, d_inner), lambda b, c: (b, c, 0)),
        scratch_shapes=[pltpu.VMEM((H, SSM_HEADDIM, SSM_STATE), F32), pltpu.VMEM((8, conv_dim), F32)],
        compiler_params=_cparams(("arbitrary", "arbitrary")),
        name="ssd",
    )(zx, dt_raw, dtt, conv_w, conv_b.reshape(1, conv_dim), dt_bias.reshape(1, H), dt_bias.reshape(H, 1),
      a_log.reshape(1, H), a_log.reshape(H, 1), d_skip.reshape(1, H), norm_g.reshape(1, d_inner))


def _outproj_route_kernel(*refs, n_y):
    x_ref = refs[0]
    y_refs = refs[1:1 + n_y]
    w_refs = refs[1 + n_y:1 + 2 * n_y]
    (g1_ref, sc_ref, sh_ref, gn_ref, rw_ref, rb_ref,
     x1_ref, h2_ref, eidx_ref, gate_ref, rank_ref, cnt_ref, carry_ref) = refs[1 + 2 * n_y:]
    i = pl.program_id(0)

    @pl.when(i == 0)
    def _():
        carry_ref[...] = jnp.zeros_like(carry_ref)

    acc = jnp.dot(y_refs[0][...], w_refs[0][...], preferred_element_type=F32)
    for yr, wr in zip(y_refs[1:], w_refs[1:]):
        acc = acc + jnp.dot(yr[...], wr[...], preferred_element_type=F32)
    x1 = x_ref[...] + g1_ref[0] * acc
    x1_ref[...] = x1
    h2 = _norm_mod(x1, gn_ref[...], sc_ref[0], sh_ref[0])
    h2b = h2.astype(BF16)
    h2_ref[...] = h2b

    tm = x1.shape[0]
    logits = jnp.dot(h2b, rw_ref[...], preferred_element_type=F32)
    scores = _sigmoid(logits)
    lane = lax.broadcasted_iota(I32, (tm, LANES), 1)
    lane_f = lane.astype(F32)
    vals = jnp.where(lane < N_EXPERTS, scores + rb_ref[...], -jnp.inf)
    eidx = jnp.zeros((tm, LANES), I32)
    gsel = jnp.zeros((tm, LANES), F32)
    oh_all = jnp.zeros((tm, LANES), F32)
    picks = []
    for k in range(TOPK):
        mx = jnp.max(vals, axis=1, keepdims=True)
        idx_f = jnp.min(jnp.where(vals == mx, lane_f, float(LANES)), axis=1, keepdims=True)
        onehot = lane_f == idx_f
        s_k = jnp.sum(jnp.where(onehot, scores, 0.0), axis=1, keepdims=True)
        vals = jnp.where(onehot, -jnp.inf, vals)
        eidx = jnp.where(lane == k, idx_f.astype(I32), eidx)
        gsel = jnp.where(lane == k, s_k, gsel)
        oh_all = oh_all + jnp.where(onehot, 1.0, 0.0)
        picks.append(onehot)
    denom = jnp.sum(gsel, axis=1, keepdims=True)
    gate_ref[...] = gsel / denom * ROUTED_SCALE
    eidx_ref[...] = eidx
    r_i = lax.broadcasted_iota(I32, (tm, tm), 0)
    c_i = lax.broadcasted_iota(I32, (tm, tm), 1)
    tri = jnp.where(c_i < r_i, 1.0, 0.0).astype(BF16)
    base = carry_ref[...] + jnp.dot(tri, oh_all.astype(BF16), preferred_element_type=F32)
    rank = jnp.zeros((tm, LANES), F32)
    for k in range(TOPK):
        r_k = jnp.sum(jnp.where(picks[k], base, 0.0), axis=1, keepdims=True)
        rank = jnp.where(lane == k, r_k, rank)
    rank_ref[...] = rank.astype(I32)
    carry = carry_ref[...] + jnp.sum(oh_all, axis=0, keepdims=True)
    carry_ref[...] = carry
    cnt_ref[...] = carry


def _outproj_route(x2, ys, ws, g1, sc2, sh2, gn, rw_pad, rb_pad, seq, tm=512):
    n, d = x2.shape
    tpb = seq // tm
    n_y = len(ys)
    bmap = lambda i: (i // tpb, 0, 0)
    row = lambda i: (i, 0)
    const = lambda i: (0, 0)
    in_specs = [pl.BlockSpec((tm, d), row)]
    in_specs += [pl.BlockSpec((tm, y.shape[1]), row) for y in ys]
    in_specs += [pl.BlockSpec(w.shape, const) for w in ws]
    in_specs += [pl.BlockSpec((1, 1, d), bmap)] * 3
    in_specs += [pl.BlockSpec((1, d), const), pl.BlockSpec((d, LANES), const), pl.BlockSpec((1, LANES), const)]
    outs = (jax.ShapeDtypeStruct((n, d), F32), jax.ShapeDtypeStruct((n, d), BF16),
            jax.ShapeDtypeStruct((n, LANES), I32), jax.ShapeDtypeStruct((n, LANES), F32),
            jax.ShapeDtypeStruct((n, LANES), I32), jax.ShapeDtypeStruct((1, LANES), F32))
    return pl.pallas_call(
        functools.partial(_outproj_route_kernel, n_y=n_y),
        out_shape=outs,
        grid=(n // tm,),
        in_specs=in_specs,
        out_specs=(pl.BlockSpec((tm, d), row), pl.BlockSpec((tm, d), row), pl.BlockSpec((tm, LANES), row),
                   pl.BlockSpec((tm, LANES), row), pl.BlockSpec((tm, LANES), row), pl.BlockSpec((1, LANES), const)),
        scratch_shapes=[pltpu.VMEM((1, LANES), F32)],
        compiler_params=_cparams(("arbitrary",)),
        name="outproj_route",
    )(x2, *ys, *ws, g1, sc2, sh2, gn.reshape(1, d), rw_pad, rb_pad)


MOE_ROWS = 256


def _dispatch_kernel(pstart_ref, cnt_ref, pad_ref, e_ref, r_ref, h_ref, z_ref, xs_ref, sem, zsem, *, tm):
    i = pl.program_id(0)

    @pl.when(i == 0)
    def _():
        def per_expert(e, tot):
            lo = pstart_ref[e] + cnt_ref[e]
            npad = pad_ref[e] - cnt_ref[e]

            def one(j, c):
                pltpu.make_async_copy(z_ref, xs_ref.at[lo + j], zsem).start()
                return c
            lax.fori_loop(0, npad, one, 0)
            return tot + npad
        total = lax.fori_loop(0, N_EXPERTS, per_expert, 0)

        def wait_one(j, c):
            pltpu.make_async_copy(z_ref, xs_ref.at[0], zsem).wait()
            return c
        lax.fori_loop(0, total, wait_one, 0)

    def per_tok(t, c):
        src = h_ref.at[t]
        for k in range(TOPK):
            a = t * TOPK + k
            dst = pstart_ref[e_ref[a]] + r_ref[a]
            pltpu.make_async_copy(src, xs_ref.at[dst], sem).start()
        return c
    lax.fori_loop(0, tm, per_tok, 0)

    def wait_tok(t, c):
        for k in range(TOPK):
            pltpu.make_async_copy(h_ref.at[0], xs_ref.at[0], sem).wait()
        return c
    lax.fori_loop(0, tm, wait_tok, 0)


def _dispatch(h3, e_flat, r_flat, pstart, counts, padded, n_slots, tm=512):
    n = h3.shape[0]
    zrow = jnp.zeros(h3.shape[1:], h3.dtype)
    smem_blk = pl.BlockSpec((tm * TOPK,), lambda i, *_: (i,), memory_space=pltpu.SMEM)
    grid_spec = pltpu.PrefetchScalarGridSpec(
        num_scalar_prefetch=3,
        grid=(n // tm,),
        in_specs=[smem_blk, smem_blk,
                  pl.BlockSpec((tm,) + h3.shape[1:], lambda i, *_: (i, 0, 0)),
                  pl.BlockSpec(zrow.shape, lambda i, *_: (0, 0))],
        out_specs=pl.BlockSpec(memory_space=pl.ANY),
        scratch_shapes=[pltpu.SemaphoreType.DMA(()), pltpu.SemaphoreType.DMA(())])
    return pl.pallas_call(
        functools.partial(_dispatch_kernel, tm=tm),
        out_shape=jax.ShapeDtypeStruct((n_slots,) + h3.shape[1:], h3.dtype),
        grid_spec=grid_spec,
        compiler_params=_cparams(("arbitrary",)),
        name="moe_dispatch",
    )(pstart, counts, padded, e_flat, r_flat, h3, zrow)


def _expert_kernel(bexp_ref, nact_ref, x_ref, w1_ref, w3_ref, w2_ref, o_ref, w1b_ref, w3b_ref, w2b_ref):
    i = pl.program_id(0)

    @pl.when(i < nact_ref[0])
    def _():
        @pl.when(jnp.logical_or(i == 0, bexp_ref[i] != bexp_ref[jnp.maximum(i - 1, 0)]))
        def _():
            w1b_ref[...] = w1_ref[0].astype(BF16)
            w3b_ref[...] = w3_ref[0].astype(BF16)
            w2b_ref[...] = w2_ref[0].astype(BF16)

        x = x_ref[...]
        a = jnp.dot(x, w1b_ref[...], preferred_element_type=F32)
        b = jnp.dot(x, w3b_ref[...], preferred_element_type=F32)
        hid = (_silu(a) * b).astype(BF16)
        o_ref[...] = jnp.dot(hid, w2b_ref[...], preferred_element_type=F32).astype(BF16)


def _experts(xs, bexp, nact, w1, w3, w2):
    p, d = xs.shape
    de = w1.shape[2]
    nb = p // MOE_ROWS
    blk = lambda i, be, na: (jnp.minimum(i, na[0] - 1), 0)
    wmap = lambda i, be, na: (be[jnp.minimum(i, na[0] - 1)], 0, 0)
    grid_spec = pltpu.PrefetchScalarGridSpec(
        num_scalar_prefetch=2,
        grid=(nb,),
        in_specs=[pl.BlockSpec((MOE_ROWS, d), blk),
                  pl.BlockSpec((1, d, de), wmap), pl.BlockSpec((1, d, de), wmap),
                  pl.BlockSpec((1, de, d), wmap)],
        out_specs=pl.BlockSpec((MOE_ROWS, d), blk),
        scratch_shapes=[pltpu.VMEM((d, de), BF16), pltpu.VMEM((d, de), BF16), pltpu.VMEM((de, d), BF16)])
    return pl.pallas_call(
        _expert_kernel,
        out_shape=jax.ShapeDtypeStruct((p, d), BF16),
        grid_spec=grid_spec,
        compiler_params=_cparams(("arbitrary",)),
        name="moe_experts",
    )(bexp, nact, xs, w1, w3, w2)


def _combine_kernel(pstart_ref, e_ref, r_ref, gt_ref, ys_ref, x1_ref, h2_ref, ws1_ref, ws3_ref, ws2_ref, g2_ref,
                    o_ref, buf_ref, routed_ref, sem, *, tm):
    def issue(t, c):
        for k in range(TOPK):
            a = t * TOPK + k
            src = pstart_ref[e_ref[a]] + r_ref[a]
            pltpu.make_async_copy(ys_ref.at[src], buf_ref.at[a], sem).start()
        return c
    lax.fori_loop(0, tm, issue, 0)

    h2 = h2_ref[...]
    a1 = jnp.dot(h2, ws1_ref[...], preferred_element_type=F32)
    a3 = jnp.dot(h2, ws3_ref[...], preferred_element_type=F32)
    shared = jnp.dot((_silu(a1) * a3).astype(BF16), ws2_ref[...], preferred_element_type=F32)

    def wait_tok(t, c):
        for k in range(TOPK):
            pltpu.make_async_copy(ys_ref.at[0], buf_ref.at[0], sem).wait()
        return c
    lax.fori_loop(0, tm, wait_tok, 0)

    nsub = buf_ref.shape[1]

    def reduce_tok(t, c):
        acc = jnp.zeros((nsub, LANES), F32)
        for k in range(TOPK):
            a = t * TOPK + k
            acc = acc + gt_ref[a] * buf_ref[a].astype(F32)
        routed_ref[pl.ds(pl.multiple_of(t * nsub, nsub), nsub), :] = acc
        return c
    lax.fori_loop(0, tm, reduce_tok, 0)

    routed = jnp.concatenate([routed_ref[pl.ds(j, tm, stride=nsub), :] for j in range(nsub)], axis=1)
    o_ref[...] = x1_ref[...] + g2_ref[0] * (routed + shared)


def _combine(ys3, e_flat, r_flat, g_flat, pstart, x1, h2, ws1, ws3, ws2, g2, seq, tm=256):
    n, d = x1.shape
    tpb = seq // tm
    nsub = ys3.shape[1]
    smem_blk = pl.BlockSpec((tm * TOPK,), lambda i, *_: (i,), memory_space=pltpu.SMEM)
    row = lambda i, *_: (i, 0)
    const = lambda i, *_: (0, 0)
    grid_spec = pltpu.PrefetchScalarGridSpec(
        num_scalar_prefetch=1,
        grid=(n // tm,),
        in_specs=[smem_blk, smem_blk, smem_blk,
                  pl.BlockSpec(memory_space=pl.ANY),
                  pl.BlockSpec((tm, d), row), pl.BlockSpec((tm, d), row),
                  pl.BlockSpec(ws1.shape, const), pl.BlockSpec(ws3.shape, const), pl.BlockSpec(ws2.shape, const),
                  pl.BlockSpec((1, 1, d), lambda i, *_: (i // tpb, 0, 0))],
        out_specs=pl.BlockSpec((tm, d), row),
        scratch_shapes=[pltpu.VMEM((tm * TOPK, nsub, LANES), ys3.dtype),
                        pltpu.VMEM((tm * nsub, LANES), F32),
                        pltpu.SemaphoreType.DMA(())])
    return pl.pallas_call(
        functools.partial(_combine_kernel, tm=tm),
        out_shape=jax.ShapeDtypeStruct((n, d), F32),
        grid_spec=grid_spec,
        compiler_params=_cparams(("arbitrary",)),
        name="moe_combine",
    )(pstart, e_flat, r_flat, g_flat, ys3, x1, h2, ws1, ws3, ws2, g2)


def _moe_plan(counts_f, n_assign):
    counts = counts_f[0, :N_EXPERTS].astype(I32)
    padded = (counts + MOE_ROWS - 1) // MOE_ROWS * MOE_ROWS
    pend = jnp.cumsum(padded)
    pstart = pend - padded
    n_blocks = (n_assign + N_EXPERTS * (MOE_ROWS - 1) + MOE_ROWS - 1) // MOE_ROWS
    blk_start = jnp.arange(n_blocks, dtype=I32) * MOE_ROWS
    bexp = jnp.minimum(jnp.sum((pend[None, :] <= blk_start[:, None]).astype(I32), axis=1), N_EXPERTS - 1)
    nact = (pend[-1:] // MOE_ROWS).astype(I32)
    return counts, padded, pstart.astype(I32), bexp, nact, n_blocks


def _moe_layer(x1, h2, eidx, gate, rank, counts_f, w1, w3, w2, ws1, ws3, ws2, g2, seq):
    n, d = x1.shape
    nsub = d // LANES
    counts, padded, pstart, bexp, nact, n_blocks = _moe_plan(counts_f, n * TOPK)
    e_flat = eidx[:, :TOPK].reshape(-1)
    r_flat = rank[:, :TOPK].reshape(-1)
    g_flat = gate[:, :TOPK].reshape(-1)
    xs3 = _dispatch(h2.reshape(n, nsub, LANES), e_flat, r_flat, pstart, counts, padded, n_blocks * MOE_ROWS)
    ys = _experts(xs3.reshape(n_blocks * MOE_ROWS, d), bexp, nact, w1, w3, w2)
    return _combine(ys.reshape(n_blocks * MOE_ROWS, nsub, LANES), e_flat, r_flat, g_flat, pstart,
                    x1, h2, ws1, ws3, ws2, g2, seq)


def kernel(x, c, positions, ada_w, ada_b, norm_mix_g, norm_ffn_g, rel_bias, even_w_in, ret_norm_g, dsa_q_g, dsa_k_g, even_w_out, ssm_w_in, ssm_conv_w, ssm_conv_b, ssm_dt_bias, ssm_a_log, ssm_d, ssm_norm_g, ssm_w_out, router_w, router_b, exp_w1, exp_w3, exp_w2, sh_w1, sh_w3, sh_w2):
    bsz, seq, d = x.shape
    n = bsz * seq
    depth = ada_w.shape[0]
    n_e = router_w.shape[2]
    d_inner = ssm_w_out.shape[1]
    xc = x.reshape(n, d)
    for i in range(depth):
        j = i // 2
        mod = _adaln_mod(c, ada_w[i], ada_b[i])
        sh1, sc1, g1, sh2, sc2, g2 = [m.reshape(bsz, 1, d) for m in jnp.split(mod, 6, axis=-1)]
        if i % 2 == 0:
            w_in = even_w_in[j]
            w_pad = jnp.pad(w_in, ((0, 0), (0, 3072 - w_in.shape[1]))).astype(BF16)
            ret_in, qa, kn, vaug, qi, kw = _inproj_even(xc, sc1, sh1, norm_mix_g[i], w_pad, dsa_k_g[j], seq)
            y_ret = _retention(ret_in.reshape(bsz, seq, 2048), positions.reshape(bsz, seq, 1), ret_norm_g[j])
            y_att = _dsa(qa.reshape(bsz, seq, 512), qi.reshape(bsz, seq, 256), kw.reshape(bsz, seq, LANES),
                         kn.reshape(bsz, seq, DSA_DH), vaug.reshape(bsz, seq, LANES), positions, rel_bias,
                         dsa_q_g[j])
            wo = even_w_out[j].astype(BF16)
            ys = [y_ret.reshape(n, 512), y_att.reshape(n, 512)]
            ws = [wo[:512], wo[512:]]
        else:
            w_in = ssm_w_in[j]
            n_main = w_in.shape[1] - d_inner // SSM_HEADDIM
            w_main = w_in[:, :n_main].astype(BF16)
            w_dt = jnp.pad(w_in[:, n_main:], ((0, 0), (0, LANES - (w_in.shape[1] - n_main)))).astype(BF16)
            zx, dt_raw = _inproj_ssm(xc, sc1, sh1, norm_mix_g[i], w_main, w_dt, seq)
            y = _ssd(zx.reshape(bsz, seq, n_main), dt_raw.reshape(bsz, seq, LANES), ssm_conv_w[j], ssm_conv_b[j],
                     ssm_dt_bias[j], ssm_a_log[j], ssm_d[j], ssm_norm_g[j], d_inner)
            ys = [y.reshape(n, d_inner)]
            ws = [ssm_w_out[j].astype(BF16)]
        rw_pad = jnp.pad(router_w[i], ((0, 0), (0, LANES - n_e))).astype(BF16)
        rb_pad = jnp.pad(router_b[i], (0, LANES - n_e)).reshape(1, LANES)
        x1, h2, eidx, gate, rank, cnt = _outproj_route(xc, ys, ws, g1, sc2, sh2, norm_ffn_g[i], rw_pad, rb_pad, seq)
        xc = _moe_layer(x1, h2, eidx, gate, rank, cnt, exp_w1[i], exp_w3[i], exp_w2[i],
                        sh_w1[i].astype(BF16), sh_w3[i].astype(BF16), sh_w2[i].astype(BF16), g2, seq)
    return xc.reshape(bsz, seq, d)
```
